```python
import math, functools
import jax, jax.numpy as jnp
from jax import lax
import numpy as np

D_MODEL = 1024
BATCH = 4
SEQ = 4096
DEPTH = 2
DEC_BATCH = 32
DEC_SEQ = 4
PAST_LEN = 16384
PAGE_SIZE = 128

POOL_WIDTH = D_MODEL // 4
POOL_WINDOWS = (2, 4, 8, 16)
POOL_GROUP = POOL_WIDTH // len(POOL_WINDOWS)
POOL_HIST = max(POOL_WINDOWS) - 1
REC_WIDTH = D_MODEL // 4
REC_BLOCKS = 4
REC_BLOCK = REC_WIDTH // REC_BLOCKS
CONV_WIDTH = 4
RGLRU_C = 8.0
N_HEADS = 4
QK_DIM = D_MODEL // 16
V_DIM = 2 * QK_DIM
ATTN_WIDTH = N_HEADS * V_DIM
ROPE_DIM = QK_DIM // 4
ROPE_THETA = 500000.0
Q_BLOCK = 128
ATTN_SCALE = 1.0 / math.sqrt(QK_DIM)
MIX_WIDTH = POOL_WIDTH + REC_WIDTH + ATTN_WIDTH
QK_WIDTH = N_HEADS * 2 * QK_DIM
IN_SPLITS = (POOL_WIDTH, POOL_WIDTH + REC_WIDTH, POOL_WIDTH + 2 * REC_WIDTH,
             POOL_WIDTH + 2 * REC_WIDTH + QK_WIDTH, POOL_WIDTH + 2 * REC_WIDTH + 2 * QK_WIDTH)
IN_WIDTH = POOL_WIDTH + 2 * REC_WIDTH + 2 * QK_WIDTH + ATTN_WIDTH
D_FF = 4 * D_MODEL
EPS = 1e-6

kernel_name = "hymba_pool_rglru_diffattn_step"


def _rmsnorm(x, g):
    xf = x.astype(jnp.float32)
    y = xf * lax.rsqrt(jnp.mean(xf * xf, axis=-1, keepdims=True) + EPS)
    return (y * g.astype(jnp.float32)).astype(x.dtype)


def _rope(x, pos):
    half = ROPE_DIM // 2
    freqs = ROPE_THETA ** (-jnp.arange(half, dtype=jnp.float32) / half)
    ang = pos.astype(jnp.float32)[:, None] * freqs[None, :]
    cos = jnp.cos(ang)[None, :, None, None, :]
    sin = jnp.sin(ang)[None, :, None, None, :]
    xf = x.astype(jnp.float32)
    x1, x2, xr = xf[..., :half], xf[..., half:ROPE_DIM], xf[..., ROPE_DIM:]
    out = jnp.concatenate([x1 * cos - x2 * sin, x1 * sin + x2 * cos, xr], axis=-1)
    return out.astype(x.dtype)


def _pool_mixer(xp_ext, pos, w, scale):
    B = xp_ext.shape[0]
    T = xp_ext.shape[1] - POOL_HIST
    xf = xp_ext.astype(jnp.float32)
    cs = jnp.concatenate([jnp.zeros_like(xf[:, :1]), jnp.cumsum(xf, axis=1)], axis=1)
    end = cs[:, POOL_HIST + 1:]
    cur = xf[:, POOL_HIST:]
    groups = []
    for g, win in enumerate(POOL_WINDOWS):
        sl = slice(g * POOL_GROUP, (g + 1) * POOL_GROUP)
        start = cs[:, POOL_HIST + 1 - win:POOL_HIST + 1 - win + T, sl]
        cnt = jnp.minimum(win, pos + 1).astype(jnp.float32)[None, :, None]
        groups.append((end[..., sl] - start) / cnt - cur[..., sl])
    d = jnp.stack(groups, axis=2).astype(xp_ext.dtype)
    y = jnp.einsum('btgc,gcd->btgd', d, w).reshape(B, T, POOL_WIDTH)
    return y * scale


def _rglru_branch(xr_ext, gate, h0, conv_w, conv_b, wa, ba, wx, bx, lam):
    B = xr_ext.shape[0]
    T = xr_ext.shape[1] - (CONV_WIDTH - 1)
    y = conv_b
    for tap in range(CONV_WIDTH):
        y = y + xr_ext[:, tap:tap + T] * conv_w[tap]
    yb = y.reshape(B, T, REC_BLOCKS, REC_BLOCK)
    r = jax.nn.sigmoid(jnp.einsum('btgc,gcd->btgd', yb, wa).reshape(B, T, REC_WIDTH) + ba)
    i = jax.nn.sigmoid(jnp.einsum('btgc,gcd->btgd', yb, wx).reshape(B, T, REC_WIDTH) + bx)
    log_a = -RGLRU_C * r.astype(jnp.float32) * jax.nn.softplus(-lam.astype(jnp.float32))
    a = jnp.exp(log_a)
    b = jnp.sqrt(-jnp.expm1(2.0 * log_a)) * (i * y).astype(jnp.float32)
    b = b.at[:, 0].add(a[:, 0] * h0.astype(jnp.float32))

    def combine(e1, e2):
        a1, b1 = e1
        a2, b2 = e2
        return a1 * a2, a2 * b1 + b2

    _, h = lax.associative_scan(combine, (a, b), axis=1)
    out = h.astype(y.dtype) * jax.nn.gelu(gate)
    return out, h[:, -1].astype(h0.dtype), xr_ext[:, -(CONV_WIDTH - 1):]


def _diff_attn_prompt(q, k, v, lam):
    B, T = q.shape[0], q.shape[1]
    nb = T // Q_BLOCK
    qb = jnp.moveaxis(q.reshape(B, nb, Q_BLOCK, N_HEADS, 2, QK_DIM), 1, 0)
    kpos = jnp.arange(T)

    def block(args):
        qi, bi = args
        s = jnp.einsum('bqhcd,bkhcd->bhcqk', qi, k).astype(jnp.float32) * ATTN_SCALE
        qpos = bi * Q_BLOCK + jnp.arange(Q_BLOCK)
        s = jnp.where(kpos[None, :] <= qpos[:, None], s, -jnp.inf)
        p = jax.nn.softmax(s, axis=-1)
        wgt = (p[:, :, 0] - lam * p[:, :, 1]).astype(v.dtype)
        return jnp.einsum('bhqk,bkhv->bqhv', wgt, v)

    o = lax.map(block, (qb, jnp.arange(nb)))
    return jnp.moveaxis(o, 0, 1).reshape(B, T, N_HEADS, V_DIM)


def _diff_attn_sample(q, k, v, lam, kp, vp):
    T = q.shape[1]
    P = kp.shape[1]
    s_past = jnp.einsum('bqhcd,bkhcd->bhcqk', q, kp).astype(jnp.float32) * ATTN_SCALE
    s_new = jnp.einsum('bqhcd,bkhcd->bhcqk', q, k).astype(jnp.float32) * ATTN_SCALE
    causal = jnp.arange(T)[None, :] <= jnp.arange(T)[:, None]
    s_new = jnp.where(causal, s_new, -jnp.inf)
    p = jax.nn.softmax(jnp.concatenate([s_past, s_new], axis=-1), axis=-1)
    wgt = (p[:, :, 0] - lam * p[:, :, 1]).astype(v.dtype)
    return (jnp.einsum('bhqk,bkhv->bqhv', wgt[..., :P], vp)
            + jnp.einsum('bhqk,bkhv->bqhv', wgt[..., P:], v))


def _layer(x, c, pos, pool_hist, conv_hist, h0, attend, lp, lam_init):
    B, T, _ = x.shape
    mod = jax.nn.silu(c) @ lp['w_ada'] + lp['b_ada']
    sh1, sc1, g1, sh2, sc2, g2 = jnp.split(mod[:, None, :], 6, axis=-1)
    h = _rmsnorm(x, lp['g_mix']) * (1.0 + sc1) + sh1
    proj = h @ lp['w_in']
    xp, xr, gate, q, k, v = jnp.split(proj, IN_SPLITS, axis=-1)
    xp_ext = jnp.concatenate([pool_hist.astype(xp.dtype), xp], axis=1)
    y_pool = _pool_mixer(xp_ext, pos, lp['pool_w'], lp['pool_scale'])
    xr_ext = jnp.concatenate([conv_hist.astype(xr.dtype), xr], axis=1)
    y_rec, h_last, conv_state = _rglru_branch(xr_ext, gate, h0, lp['conv_w'], lp['conv_b'],
                                              lp['rg_wa'], lp['rg_ba'], lp['rg_wx'], lp['rg_bx'],
                                              lp['rg_lambda'])
    q = _rope(_rmsnorm(q.reshape(B, T, N_HEADS, 2, QK_DIM), lp['q_norm']), pos)
    k = _rope(_rmsnorm(k.reshape(B, T, N_HEADS, 2, QK_DIM), lp['k_norm']), pos)
    v = v.reshape(B, T, N_HEADS, V_DIM)
    lam = (jnp.exp(jnp.sum(lp['lambda_q1'].astype(jnp.float32) * lp['lambda_k1'].astype(jnp.float32)))
           - jnp.exp(jnp.sum(lp['lambda_q2'].astype(jnp.float32) * lp['lambda_k2'].astype(jnp.float32)))
           + lam_init)
    o = attend(q, k, v, lam)
    o = _rmsnorm(o, lp['subln']) * (1.0 - lam_init)
    mix = jnp.concatenate([y_pool, y_rec, o.reshape(B, T, ATTN_WIDTH)], axis=-1) @ lp['w_out']
    x = x + g1 * mix
    h2 = _rmsnorm(x, lp['g_mlp']) * (1.0 + sc2) + sh2
    x = x + g2 * (jnp.square(jax.nn.relu(h2 @ lp['w_up'])) @ lp['w_down'])
    return x, k, v, xp_ext[:, -POOL_HIST:], conv_state, h_last


def setup_inputs(seed: int = 0) -> dict:
    key = jax.random.key(seed)
    ks = jax.random.split(key, 40)
    f32 = jnp.float32
    n_pages = PAST_LEN // PAGE_SIZE
    n_used = DEC_BATCH * n_pages
    n_pool = n_used + max(1, n_used // 4)
    nrm = lambda k, shape, s=1.0: jax.random.normal(k, shape, f32) * s
    u = jax.random.uniform(ks[20], (DEPTH, REC_WIDTH), f32, minval=0.9, maxval=0.999)
    return {
        'x_prompt': nrm(ks[0], (BATCH, SEQ, D_MODEL)),
        'x_sample': nrm(ks[1], (DEC_BATCH, DEC_SEQ, D_MODEL)),
        'cache_k': nrm(ks[2], (DEPTH, n_pool, PAGE_SIZE, N_HEADS, 2, QK_DIM)),
        'cache_v': nrm(ks[3], (DEPTH, n_pool, PAGE_SIZE, N_HEADS, V_DIM)),
        'page_table': jax.random.permutation(ks[4], n_pool)[:n_used].reshape(DEC_BATCH, n_pages).astype(jnp.int32),
        'state_pool': nrm(ks[5], (DEPTH, DEC_BATCH, POOL_HIST, POOL_WIDTH)),
        'state_conv': nrm(ks[6], (DEPTH, DEC_BATCH, CONV_WIDTH - 1, REC_WIDTH)),
        'state_rglru': nrm(ks[7], (DEPTH, DEC_BATCH, REC_WIDTH), 0.5),
        'c_prompt': nrm(ks[8], (BATCH, D_MODEL)),
        'c_sample': nrm(ks[9], (DEC_BATCH, D_MODEL)),
        'w_ada': nrm(ks[10], (DEPTH, D_MODEL, 6 * D_MODEL), 0.5 * D_MODEL ** -0.5),
        'b_ada': nrm(ks[11], (DEPTH, 6 * D_MODEL), 0.01),
        'g_mix': 1.0 + nrm(ks[12], (DEPTH, D_MODEL), 0.02),
        'w_in': nrm(ks[13], (DEPTH, D_MODEL, IN_WIDTH), D_MODEL ** -0.5),
        'pool_w': nrm(ks[14], (DEPTH, len(POOL_WINDOWS), POOL_GROUP, POOL_GROUP), POOL_GROUP ** -0.5),
        'pool_scale': 1.0 + nrm(ks[15], (DEPTH, POOL_WIDTH), 0.1),
        'conv_w': nrm(ks[16], (DEPTH, CONV_WIDTH, REC_WIDTH), CONV_WIDTH ** -0.5),
        'conv_b': nrm(ks[17], (DEPTH, REC_WIDTH), 0.01),
        'rg_wa': nrm(ks[18], (DEPTH, REC_BLOCKS, REC_BLOCK, REC_BLOCK), REC_BLOCK ** -0.5),
        'rg_ba': nrm(ks[19], (DEPTH, REC_WIDTH), 0.01),
        'rg_wx': nrm(ks[21], (DEPTH, REC_BLOCKS, REC_BLOCK, REC_BLOCK), REC_BLOCK ** -0.5),
        'rg_bx': nrm(ks[22], (DEPTH, REC_WIDTH), 0.01),
        'rg_lambda': jnp.log(u) - jnp.log1p(-u),
        'q_norm': 1.0 + nrm(ks[23], (DEPTH, QK_DIM), 0.02),
        'k_norm': 1.0 + nrm(ks[24], (DEPTH, QK_DIM), 0.02),
        'lambda_q1': nrm(ks[25], (DEPTH, QK_DIM), 0.1),
        'lambda_k1': nrm(ks[26], (DEPTH, QK_DIM), 0.1),
        'lambda_q2': nrm(ks[27], (DEPTH, QK_DIM), 0.1),
        'lambda_k2': nrm(ks[28], (DEPTH, QK_DIM), 0.1),
        'subln': 1.0 + nrm(ks[29], (DEPTH, V_DIM), 0.02),
        'w_out': nrm(ks[30], (DEPTH, MIX_WIDTH, D_MODEL), MIX_WIDTH ** -0.5),
        'g_mlp': 1.0 + nrm(ks[31], (DEPTH, D_MODEL), 0.02),
        'w_up': nrm(ks[32], (DEPTH, D_MODEL, D_FF), D_MODEL ** -0.5),
        'w_down': nrm(ks[33], (DEPTH, D_FF, D_MODEL), D_FF ** -0.5),
    }


def reference(x_prompt, x_sample, cache_k, cache_v, page_table, state_pool, state_conv, state_rglru,
              c_prompt, c_sample, w_ada, b_ada, g_mix, w_in, pool_w, pool_scale, conv_w, conv_b,
              rg_wa, rg_ba, rg_wx, rg_bx, rg_lambda, q_norm, k_norm, lambda_q1, lambda_k1,
              lambda_q2, lambda_k2, subln, w_out, g_mlp, w_up, w_down):
    n_pages = PAST_LEN // PAGE_SIZE
    pos_p = jnp.arange(SEQ)
    pos_s = PAST_LEN + jnp.arange(DEC_SEQ)
    xp, xs = x_prompt, x_sample
    zpool = jnp.zeros((BATCH, POOL_HIST, POOL_WIDTH), x_prompt.dtype)
    zconv = jnp.zeros((BATCH, CONV_WIDTH - 1, REC_WIDTH), x_prompt.dtype)
    zh = jnp.zeros((BATCH, REC_WIDTH), x_prompt.dtype)
    kP, vP, kS, vS, poolP, poolS, convP, convS, hP, hS = ([] for _ in range(10))
    for l in range(DEPTH):
        lp = dict(w_ada=w_ada[l], b_ada=b_ada[l], g_mix=g_mix[l], w_in=w_in[l], pool_w=pool_w[l],
                  pool_scale=pool_scale[l], conv_w=conv_w[l], conv_b=conv_b[l], rg_wa=rg_wa[l],
                  rg_ba=rg_ba[l], rg_wx=rg_wx[l], rg_bx=rg_bx[l], rg_lambda=rg_lambda[l],
                  q_norm=q_norm[l], k_norm=k_norm[l], lambda_q1=lambda_q1[l], lambda_k1=lambda_k1[l],
                  lambda_q2=lambda_q2[l], lambda_k2=lambda_k2[l], subln=subln[l], w_out=w_out[l],
                  g_mlp=g_mlp[l], w_up=w_up[l], w_down=w_down[l])
        lam_init = 0.8 - 0.6 * math.exp(-0.3 * l)
        xp, k1, v1, p1, c1, h1 = _layer(xp, c_prompt, pos_p, zpool, zconv, zh,
                                        _diff_attn_prompt, lp, lam_init)
        kp = cache_k[l, page_table].reshape(DEC_BATCH, n_pages * PAGE_SIZE, N_HEADS, 2, QK_DIM)
        vp = cache_v[l, page_table].reshape(DEC_BATCH, n_pages * PAGE_SIZE, N_HEADS, V_DIM)
        attend_s = functools.partial(_diff_attn_sample, kp=kp, vp=vp)
        xs, k2, v2, p2, c2, h2 = _layer(xs, c_sample, pos_s, state_pool[l], state_conv[l],
                                        state_rglru[l], attend_s, lp, lam_init)
        kP.append(k1); vP.append(v1); poolP.append(p1); convP.append(c1); hP.append(h1)
        kS.append(k2); vS.append(v2); poolS.append(p2); convS.append(c2); hS.append(h2)
    return (xp, xs, jnp.stack(kP), jnp.stack(vP), jnp.stack(kS), jnp.stack(vS),
            jnp.stack(poolP), jnp.stack(poolS), jnp.stack(convP), jnp.stack(convS),
            jnp.stack(hP), jnp.stack(hS))
```

```python
import functools
import math

import jax
import jax.numpy as jnp
from jax import lax
from jax.experimental import pallas as pl
from jax.experimental.pallas import tpu as pltpu

D_MODEL = 1024
BATCH = 4
SEQ = 4096
DEPTH = 2
DEC_BATCH = 32
DEC_SEQ = 4
PAST_LEN = 16384
PAGE_SIZE = 128
N_PAGES = PAST_LEN // PAGE_SIZE

POOL_WIDTH = 256
POOL_WINDOWS = (2, 4, 8, 16)
POOL_GROUP = 64
POOL_HIST = 15
REC_WIDTH = 256
CONV_WIDTH = 4
RGLRU_C = 8.0
N_HEADS = 4
QK_DIM = 64
V_DIM = 128
ATTN_WIDTH = N_HEADS * V_DIM
ROPE_DIM = 16
ROPE_THETA = 500000.0
ATTN_SCALE = 1.0 / math.sqrt(QK_DIM)
QK_WIDTH = N_HEADS * 2 * QK_DIM
IN_WIDTH = POOL_WIDTH + 2 * REC_WIDTH + 2 * QK_WIDTH + ATTN_WIDTH
D_FF = 4 * D_MODEL
EPS = 1e-6

F32 = jnp.float32
BF16 = jnp.bfloat16
NEG_BIG = -1e30
V7X_VMEM_LIMIT = 56 * 1024 * 1024

PROJ_TM = 512
MIX_TT = 512
ATTN_TQ = 512
PAGES_PER_STEP = 8
FF_CHUNK = 1024


def _bdot(a, b):
    return jnp.dot(a.astype(BF16), b.astype(BF16), preferred_element_type=F32)


def _params(n_axes):
    return pltpu.CompilerParams(dimension_semantics=("arbitrary",) * n_axes,
                                vmem_limit_bytes=V7X_VMEM_LIMIT)


def _ada_kernel(c_ref, w_ref, b_ref, o_ref):
    c = c_ref[...]
    o_ref[...] = _bdot(c * jax.nn.sigmoid(c), w_ref[...]) + b_ref[...]


def _ada_call(c_all, w_ada, b_ada):
    rows = c_all.shape[0]
    tn = 1536
    return pl.pallas_call(
        _ada_kernel,
        grid=(6 * D_MODEL // tn,),
        in_specs=[pl.BlockSpec((rows, D_MODEL), lambda j: (0, 0)),
                  pl.BlockSpec((D_MODEL, tn), lambda j: (0, j)),
                  pl.BlockSpec((1, tn), lambda j: (0, j))],
        out_specs=pl.BlockSpec((rows, tn), lambda j: (0, j)),
        out_shape=jax.ShapeDtypeStruct((rows, 6 * D_MODEL), F32),
        compiler_params=_params(1),
        name="ada",
    )(c_all, w_ada, b_ada.reshape(1, -1))


def _proj_kernel(x_ref, mod_ref, gmix_ref, w_ref, cos_ref, sin_ref, qg_ref, kg_ref, grp_ref,
                 xp_ref, xr_ref, gate_ref, q_ref, k_ref, v_ref):
    x = x_ref[...]
    y = x * lax.rsqrt(jnp.mean(x * x, axis=-1, keepdims=True) + EPS) * gmix_ref[...]
    sh1 = mod_ref[:, 0:D_MODEL]
    sc1 = mod_ref[:, D_MODEL:2 * D_MODEL]
    h = y * (1.0 + sc1) + sh1
    proj = _bdot(h, w_ref[...])
    xp_ref[...] = proj[:, 0:256]
    xr_ref[...] = proj[:, 256:512]
    gate_ref[...] = proj[:, 512:768]
    v_ref[...] = proj[:, 1792:2304]

    cos = jnp.concatenate([cos_ref[...]] * N_HEADS, axis=1)
    sin = jnp.concatenate([sin_ref[...]] * N_HEADS, axis=1)
    lane = lax.broadcasted_iota(jnp.int32, cos.shape, 1) % QK_DIM
    grp = grp_ref[...]

    def norm_rope(z, g):
        z2 = z * z
        hi = z2.astype(BF16)
        lo = (z2 - hi.astype(F32)).astype(BF16)
        ssq = (jnp.dot(hi, grp, preferred_element_type=F32)
               + jnp.dot(lo, grp, preferred_element_type=F32))
        zn = z * lax.rsqrt(ssq * (1.0 / QK_DIM) + EPS) * g
        half = ROPE_DIM // 2
        swapped = jnp.where(lane < half, pltpu.roll(zn, QK_WIDTH - half, 1), pltpu.roll(zn, half, 1))
        return zn * cos + swapped * sin

    q = norm_rope(proj[:, 768:1280], qg_ref[...])
    q_ref[...] = (q * ATTN_SCALE).astype(q_ref.dtype)
    k_ref[...] = norm_rope(proj[:, 1280:1792], kg_ref[...])


def _proj_call(x, mod3, gmix, w_in_bf, cos_t, sin_t, qg, kg, grp, *, tm, tiles_per_seq):
    m = x.shape[0]
    mod_rows = mod3.shape[1]
    row = lambda w: pl.BlockSpec((tm, w), lambda i: (i, 0))
    full = lambda a: pl.BlockSpec(a.shape, lambda i: (0,) * a.ndim)
    return pl.pallas_call(
        _proj_kernel,
        grid=(m // tm,),
        in_specs=[row(D_MODEL),
                  pl.BlockSpec((None, mod_rows, 6 * D_MODEL), lambda i: (i // tiles_per_seq, 0, 0)),
                  full(gmix), full(w_in_bf),
                  pl.BlockSpec((tm, V_DIM), lambda i: (i % tiles_per_seq, 0)),
                  pl.BlockSpec((tm, V_DIM), lambda i: (i % tiles_per_seq, 0)),
                  full(qg), full(kg), full(grp)],
        out_specs=[row(256), row(256), row(256), row(512), row(512), row(512)],
        out_shape=[jax.ShapeDtypeStruct((m, 256), F32)] * 3
                  + [jax.ShapeDtypeStruct((m, 512), BF16)]
                  + [jax.ShapeDtypeStruct((m, 512), F32)] * 2,
        compiler_params=_params(1),
        name="proj",
    )(x, mod3, gmix, w_in_bf, cos_t, sin_t, qg, kg, grp)


def _mix_kernel(*refs, stride, rows, pos0, has_hist):
    if has_hist:
        (xp_ref, xr_ref, gate_ref, hp_ref, hc_ref, h0_ref, poolw_ref, pscale_ref, convw_ref, convb_ref,
         wax_ref, bax_ref, lam_ref, ypool_ref, yrec_ref, hlast_ref, pext, cext, hcar) = refs
    else:
        (xp_ref, xr_ref, gate_ref, poolw_ref, pscale_ref, convw_ref, convb_ref,
         wax_ref, bax_ref, lam_ref, ypool_ref, yrec_ref, hlast_ref, pext, cext, hcar) = refs
    t = pl.program_id(1)
    tt = rows // stride
    hp_rows = 16 * stride
    hc_rows = 8 * stride

    @pl.when(t == 0)
    def _():
        pext[0:hp_rows, :] = jnp.zeros((hp_rows, POOL_WIDTH), F32)
        cext[0:hc_rows, :] = jnp.zeros((hc_rows, REC_WIDTH), F32)
        if has_hist:
            pext[stride:hp_rows, :] = hp_ref[...]
            cext[(8 - (CONV_WIDTH - 1)) * stride:hc_rows, :] = hc_ref[...]
            hcar[...] = h0_ref[...]
        else:
            hcar[...] = jnp.zeros((stride, REC_WIDTH), F32)

    @pl.when(t > 0)
    def _():
        pext[0:hp_rows, :] = pext[rows:rows + hp_rows, :]
        cext[0:hc_rows, :] = cext[rows:rows + hc_rows, :]

    pext[hp_rows:hp_rows + rows, :] = xp_ref[...]
    cext[hc_rows:hc_rows + rows, :] = xr_ref[...]

    tidx = lax.broadcasted_iota(jnp.int32, (rows, 128), 0) // stride
    lane = lax.broadcasted_iota(jnp.int32, (rows, 128), 1)
    pos1 = (pos0 + 1 + t * tt + tidx).astype(F32)

    def back(j, lo):
        return pext[hp_rows - j * stride:hp_rows - j * stride + rows, lo:lo + 128]

    def window_sums(lo, w_small):
        acc = back(0, lo)
        for j in range(1, w_small):
            acc = acc + back(j, lo)
        big = acc
        for j in range(w_small, 2 * w_small):
            big = big + back(j, lo)
        cnt = jnp.where(lane < POOL_GROUP, jnp.minimum(float(w_small), pos1),
                        jnp.minimum(float(2 * w_small), pos1))
        return jnp.where(lane < POOL_GROUP, acc, big) / cnt - back(0, lo)

    d = jnp.concatenate([window_sums(0, 2), window_sums(128, 8)], axis=1)
    ypool_ref[...] = (_bdot(d, poolw_ref[...]) * pscale_ref[...]).astype(ypool_ref.dtype)

    y = convb_ref[...]
    for tap in range(CONV_WIDTH):
        off = hc_rows - (CONV_WIDTH - 1 - tap) * stride
        y = y + cext[off:off + rows, :] * convw_ref[tap:tap + 1, :]
    rg = _bdot(y, wax_ref[...]) + bax_ref[...]
    r = jax.nn.sigmoid(rg[:, 0:REC_WIDTH])
    i = jax.nn.sigmoid(rg[:, REC_WIDTH:2 * REC_WIDTH])
    nl = -lam_ref[...]
    softplus = jnp.maximum(nl, 0.0) + jnp.log1p(jnp.exp(-jnp.abs(nl)))
    a = jnp.exp(-RGLRU_C * r * softplus)
    b = jnp.sqrt(1.0 - a * a) * (i * y)

    tfull = lax.broadcasted_iota(jnp.int32, (rows, REC_WIDTH), 0) // stride
    step = 1
    while step < tt:
        a_prev = pltpu.roll(a, step * stride, 0)
        b_prev = pltpu.roll(b, step * stride, 0)
        valid = tfull >= step
        b = jnp.where(valid, a * b_prev + b, b)
        a = jnp.where(valid, a * a_prev, a)
        step *= 2
    hc = hcar[...]
    carry = jnp.broadcast_to(hc, (rows, REC_WIDTH)) if stride == 1 else jnp.concatenate([hc] * tt, axis=0)
    h = b + a * carry
    h_end = h[rows - stride:rows, :]
    hcar[...] = h_end
    hlast_ref[...] = h_end

    g = gate_ref[...]
    gelu = 0.5 * g * (1.0 + jnp.tanh(math.sqrt(2.0 / math.pi) * (g + 0.044715 * (g * g * g))))
    yrec_ref[...] = (h * gelu).astype(yrec_ref.dtype)


def _mix_call(xp, xr, gate, hist, wts, *, n_seq, n_t, stride, rows, pos0):
    m = xp.shape[0]
    has_hist = hist is not None
    row = pl.BlockSpec((rows, 256), lambda b, t: (b * n_t + t, 0))
    full = lambda a: pl.BlockSpec(a.shape, lambda b, t: (0,) * a.ndim)
    ins = [xp, xr, gate] + (list(hist) if has_hist else []) + list(wts)
    in_specs = [row, row, row] + [full(a) for a in ins[3:]]
    return pl.pallas_call(
        functools.partial(_mix_kernel, stride=stride, rows=rows, pos0=pos0, has_hist=has_hist),
        grid=(n_seq, n_t),
        in_specs=in_specs,
        out_specs=[row, row, pl.BlockSpec((None, stride, REC_WIDTH), lambda b, t: (b, 0, 0))],
        out_shape=[jax.ShapeDtypeStruct((m, 256), BF16), jax.ShapeDtypeStruct((m, 256), BF16),
                   jax.ShapeDtypeStruct((n_seq, stride, REC_WIDTH), F32)],
        scratch_shapes=[pltpu.VMEM((16 * stride + rows, POOL_WIDTH), F32),
                        pltpu.VMEM((8 * stride + rows, REC_WIDTH), F32),
                        pltpu.VMEM((stride, REC_WIDTH), F32)],
        compiler_params=_params(2),
        name="mix",
    )(*ins)


def _diff_lambda(lamp_ref, lam_init):
    lp = lamp_ref[...]
    s1 = jnp.sum(lp[0:1, :] * lp[1:2, :], axis=-1, keepdims=True)
    s2 = jnp.sum(lp[2:3, :] * lp[3:4, :], axis=-1, keepdims=True)
    return jnp.exp(s1) - jnp.exp(s2) + lam_init


def _subln(o, g, lam_init):
    return o * lax.rsqrt(jnp.mean(o * o, axis=-1, keepdims=True) + EPS) * g * (1.0 - lam_init)


def _attn_prompt_kernel(q_ref, k_ref, v_ref, lamp_ref, subln_ref, o_ref, kb, vb, q2, m_s, acc_s, *, tq, lam_init):
    qi = pl.program_id(2)

    @pl.when(qi == 0)
    def _():
        kb[...] = k_ref[...].astype(BF16)
        vb[:, 0:V_DIM] = v_ref[...].astype(BF16)
        vb[:, V_DIM:2 * V_DIM] = jnp.ones((SEQ, V_DIM), BF16)

    q = q_ref[...]
    lane = lax.broadcasted_iota(jnp.int32, (tq, 2 * QK_DIM), 1)
    zero = jnp.zeros_like(q)
    q2[0:tq, :] = jnp.where(lane < QK_DIM, q, zero)
    q2[tq:2 * tq, :] = jnp.where(lane >= QK_DIM, q, zero)
    m_s[...] = jnp.full((2 * tq, 1), NEG_BIG, F32)
    acc_s[...] = jnp.zeros((2 * tq, 2 * V_DIM), F32)

    def step(kj, masked):
        start = pl.multiple_of(kj * tq, tq)
        s = lax.dot_general(q2[...], kb[pl.ds(start, tq), :], (((1,), (1,)), ((), ())),
                            preferred_element_type=F32)
        if masked:
            r = lax.broadcasted_iota(jnp.int32, (2 * tq, tq), 0)
            r = jnp.where(r >= tq, r - tq, r)
            c = lax.broadcasted_iota(jnp.int32, (2 * tq, tq), 1)
            s = jnp.where(c <= r, s, NEG_BIG)
        m_old = m_s[...]
        m_new = jnp.maximum(m_old, jnp.max(s, axis=-1, keepdims=True))
        e = jnp.exp(s - m_new).astype(BF16)
        pv = jnp.dot(e, vb[pl.ds(start, tq), :], preferred_element_type=F32)
        acc_s[...] = jnp.exp(m_old - m_new) * acc_s[...] + pv
        m_s[...] = m_new

    def body(kj, carry):
        step(kj, False)
        return carry

    lax.fori_loop(0, qi, body, 0)
    step(qi, True)

    acc = acc_s[...]
    a0 = acc[0:tq, :]
    a1 = acc[tq:2 * tq, :]
    lam = _diff_lambda(lamp_ref, lam_init)
    o = a0[:, 0:V_DIM] / a0[:, V_DIM:V_DIM + 1] - lam * (a1[:, 0:V_DIM] / a1[:, V_DIM:V_DIM + 1])
    o_ref[...] = _subln(o, subln_ref[...], lam_init).astype(o_ref.dtype)


def _attn_prompt_call(q, k, v, lamp, subln, *, lam_init):
    tq = ATTN_TQ
    qspec = pl.BlockSpec((None, tq, V_DIM), lambda b, h, i: (b, i, h))
    kvspec = pl.BlockSpec((None, SEQ, V_DIM), lambda b, h, i: (b, 0, h))
    full = lambda a: pl.BlockSpec(a.shape, lambda b, h, i: (0,) * a.ndim)
    return pl.pallas_call(
        functools.partial(_attn_prompt_kernel, tq=tq, lam_init=lam_init),
        grid=(BATCH, N_HEADS, SEQ // tq),
        in_specs=[qspec, kvspec, kvspec, full(lamp), full(subln)],
        out_specs=qspec,
        out_shape=jax.ShapeDtypeStruct((BATCH, SEQ, ATTN_WIDTH), BF16),
        scratch_shapes=[pltpu.VMEM((SEQ, 2 * QK_DIM), BF16),
                        pltpu.VMEM((SEQ, 2 * V_DIM), BF16),
                        pltpu.VMEM((2 * tq, 2 * QK_DIM), BF16),
                        pltpu.VMEM((2 * tq, 1), F32),
                        pltpu.VMEM((2 * tq, 2 * V_DIM), F32)],
        compiler_params=_params(3),
        name="attn_prompt",
    )(q, k, v, lamp, subln)


N_SROWS = DEC_SEQ * N_HEADS * 2


def _attn_sample_kernel(pt_ref, q_ref, kn_ref, vn_ref, lamp_ref, subln_ref, *refs, pages, lam_init):
    k_refs = refs[0:pages]
    v_refs = refs[pages:2 * pages]
    o_ref = refs[2 * pages]
    qbd, knp, vnp, m_s, l_s, acc_s = refs[2 * pages + 1:]
    p = pl.program_id(1)
    n_steps = pl.num_programs(1)

    @pl.when(p == 0)
    def _():
        q = q_ref[...]
        sub = lax.broadcasted_iota(jnp.int32, (8, QK_WIDTH), 0)
        grp = lax.broadcasted_iota(jnp.int32, (8, QK_WIDTH), 1) // QK_DIM
        blocks = [jnp.where(sub == grp, jnp.broadcast_to(q[t:t + 1, :], (8, QK_WIDTH)), 0.0)
                  for t in range(DEC_SEQ)]
        qbd[...] = jnp.concatenate(blocks, axis=0).astype(BF16)
        knp[...] = jnp.zeros((PAGE_SIZE, QK_WIDTH), F32)
        vnp[...] = jnp.zeros((PAGE_SIZE, ATTN_WIDTH), F32)
        knp[0:DEC_SEQ, :] = kn_ref[...]
        vnp[0:DEC_SEQ, :] = vn_ref[...]
        m_s[...] = jnp.full((N_SROWS, 1), NEG_BIG, F32)
        l_s[...] = jnp.zeros((N_SROWS, 1), F32)
        acc_s[...] = jnp.zeros((N_SROWS, ATTN_WIDTH), F32)

    def scores(kpage):
        return lax.dot_general(qbd[...], kpage.astype(BF16), (((1,), (1,)), ((), ())),
                               preferred_element_type=F32)

    def accumulate(s, vpages):
        m_old = m_s[...]
        m_new = jnp.maximum(m_old, jnp.max(s, axis=-1, keepdims=True))
        alpha = jnp.exp(m_old - m_new)
        e = jnp.exp(s - m_new)
        l_s[...] = alpha * l_s[...] + jnp.sum(e, axis=-1, keepdims=True)
        eb = e.astype(BF16)
        pv = None
        for j, vp in enumerate(vpages):
            term = jnp.dot(eb[:, j * PAGE_SIZE:(j + 1) * PAGE_SIZE], vp.astype(BF16),
                           preferred_element_type=F32)
            pv = term if pv is None else pv + term
        acc_s[...] = alpha * acc_s[...] + pv
        m_s[...] = m_new

    s_past = jnp.concatenate([scores(k_refs[j][...]) for j in range(pages)], axis=1)
    accumulate(s_past, [v_refs[j][...] for j in range(pages)])

    @pl.when(p == n_steps - 1)
    def _():
        s_new = scores(knp[...])
        row_t = lax.broadcasted_iota(jnp.int32, (N_SROWS, PAGE_SIZE), 0) // 8
        col = lax.broadcasted_iota(jnp.int32, (N_SROWS, PAGE_SIZE), 1)
        accumulate(jnp.where(col <= row_t, s_new, NEG_BIG), [vnp[...]])
        on = acc_s[...] / l_s[...]
        lam = _diff_lambda(lamp_ref, lam_init)
        sub = lax.broadcasted_iota(jnp.int32, (8, ATTN_WIDTH), 0)
        head = lax.broadcasted_iota(jnp.int32, (8, ATTN_WIDTH), 1) // V_DIM
        coef = jnp.where(sub == 2 * head, 1.0, 0.0) - jnp.where(sub == 2 * head + 1, 1.0, 0.0) * lam
        g = subln_ref[...]
        for t in range(DEC_SEQ):
            o_t = jnp.sum(on[8 * t:8 * t + 8, :] * coef, axis=0, keepdims=True)
            for h in range(N_HEADS):
                o_ref[t:t + 1, h * V_DIM:(h + 1) * V_DIM] = _subln(
                    o_t[:, h * V_DIM:(h + 1) * V_DIM], g, lam_init)


def _attn_sample_call(page_table, q_bm, k_bm, v_bm, lamp, subln, cache_k4, cache_v4, *, layer, lam_init):
    pages = PAGES_PER_STEP
    n_steps = N_PAGES // pages
    tok = pl.BlockSpec((None, DEC_SEQ, QK_WIDTH), lambda b, p, pt: (b, 0, 0))
    full = lambda a: pl.BlockSpec(a.shape, lambda b, p, pt: (0,) * a.ndim)

    def page_spec(j):
        return pl.BlockSpec((None, None, PAGE_SIZE, QK_WIDTH),
                            lambda b, p, pt: (layer, pt[b, p * pages + j], 0, 0))

    grid_spec = pltpu.PrefetchScalarGridSpec(
        num_scalar_prefetch=1,
        grid=(DEC_BATCH, n_steps),
        in_specs=[tok, tok, tok, full(lamp), full(subln)]
                 + [page_spec(j) for j in range(pages)] * 2,
        out_specs=tok,
        scratch_shapes=[pltpu.VMEM((N_SROWS, QK_WIDTH), BF16),
                        pltpu.VMEM((PAGE_SIZE, QK_WIDTH), F32),
                        pltpu.VMEM((PAGE_SIZE, ATTN_WIDTH), F32),
                        pltpu.VMEM((N_SROWS, 1), F32),
                        pltpu.VMEM((N_SROWS, 1), F32),
                        pltpu.VMEM((N_SROWS, ATTN_WIDTH), F32)])
    return pl.pallas_call(
        functools.partial(_attn_sample_kernel, pages=pages, lam_init=lam_init),
        grid_spec=grid_spec,
        out_shape=jax.ShapeDtypeStruct((DEC_BATCH, DEC_SEQ, ATTN_WIDTH), F32),
        compiler_params=_params(2),
        name="attn_sample",
    )(page_table, q_bm, k_bm, v_bm, lamp, subln, *([cache_k4] * pages), *([cache_v4] * pages))


def _out_kernel(x_ref, yp_ref, yr_ref, o_ref, mod_ref, wout_ref, gmlp_ref, wup_ref, wdown_ref, xo_ref):
    mix = (jnp.dot(yp_ref[...].astype(BF16), wout_ref[0:256, :], preferred_element_type=F32)
           + jnp.dot(yr_ref[...].astype(BF16), wout_ref[256:512, :], preferred_element_type=F32)
           + jnp.dot(o_ref[...].astype(BF16), wout_ref[512:1024, :], preferred_element_type=F32))
    g1 = mod_ref[:, 2 * D_MODEL:3 * D_MODEL]
    sh2 = mod_ref[:, 3 * D_MODEL:4 * D_MODEL]
    sc2 = mod_ref[:, 4 * D_MODEL:5 * D_MODEL]
    g2 = mod_ref[:, 5 * D_MODEL:6 * D_MODEL]
    x1 = x_ref[...] + g1 * mix
    y = x1 * lax.rsqrt(jnp.mean(x1 * x1, axis=-1, keepdims=True) + EPS) * gmlp_ref[...]
    h2 = (y * (1.0 + sc2) + sh2).astype(BF16)
    mlp = None
    for c in range(D_FF // FF_CHUNK):
        up = jnp.dot(h2, wup_ref[:, c * FF_CHUNK:(c + 1) * FF_CHUNK], preferred_element_type=F32)
        act = jnp.square(jnp.maximum(up, 0.0)).astype(BF16)
        term = jnp.dot(act, wdown_ref[c * FF_CHUNK:(c + 1) * FF_CHUNK, :], preferred_element_type=F32)
        mlp = term if mlp is None else mlp + term
    xo_ref[...] = x1 + g2 * mlp


def _out_call(x, yp, yr, o, mod3, wout_bf, gmlp, wup_bf, wdown_bf, *, tm, tiles_per_seq):
    m = x.shape[0]
    mod_rows = mod3.shape[1]
    row = lambda w: pl.BlockSpec((tm, w), lambda i: (i, 0))
    full = lambda a: pl.BlockSpec(a.shape, lambda i: (0,) * a.ndim, pipeline_mode=pl.Buffered(1))
    return pl.pallas_call(
        _out_kernel,
        grid=(m // tm,),
        in_specs=[row(D_MODEL), row(256), row(256), row(512),
                  pl.BlockSpec((None, mod_rows, 6 * D_MODEL), lambda i: (i // tiles_per_seq, 0, 0)),
                  full(wout_bf), full(gmlp), full(wup_bf), full(wdown_bf)],
        out_specs=row(D_MODEL),
        out_shape=jax.ShapeDtypeStruct((m, D_MODEL), F32),
        compiler_params=_params(1),
        name="out",
    )(x, yp, yr, o, mod3, wout_bf, gmlp, wup_bf, wdown_bf)


def _block_diag(w):
    g, c, d = w.shape
    eye = jnp.eye(g, dtype=w.dtype)
    return (eye[:, None, :, None] * w[:, :, None, :]).reshape(g * c, g * d)


def _rope_tables(pos):
    half = ROPE_DIM // 2
    freqs = ROPE_THETA ** (-jnp.arange(half, dtype=F32) / half)
    ang = pos.astype(F32)[:, None] * freqs[None, :]
    cos, sin = jnp.cos(ang), jnp.sin(ang)
    n = pos.shape[0]
    pad1 = jnp.ones((n, QK_DIM - ROPE_DIM), F32)
    pad0 = jnp.zeros((n, QK_DIM - ROPE_DIM), F32)
    c64 = jnp.concatenate([cos, cos, pad1], axis=1)
    s64 = jnp.concatenate([-sin, sin, pad0], axis=1)
    return jnp.concatenate([c64, c64], axis=1), jnp.concatenate([s64, s64], axis=1)


def kernel(x_prompt, x_sample, cache_k, cache_v, page_table, state_pool, state_conv, state_rglru,
           c_prompt, c_sample, w_ada, b_ada, g_mix, w_in, pool_w, pool_scale, conv_w, conv_b,
           rg_wa, rg_ba, rg_wx, rg_bx, rg_lambda, q_norm, k_norm, lambda_q1, lambda_k1,
           lambda_q2, lambda_k2, subln, w_out, g_mlp, w_up, w_down):
    n_pool = cache_k.shape[1]
    mp = BATCH * SEQ
    ms = DEC_BATCH * DEC_SEQ
    tm_tiles = SEQ // PROJ_TM

    cos_p, sin_p = _rope_tables(jnp.arange(SEQ))
    cos_s, sin_s = _rope_tables(PAST_LEN + jnp.repeat(jnp.arange(DEC_SEQ), DEC_BATCH))
    grp = _block_diag(jnp.ones((QK_WIDTH // QK_DIM, QK_DIM, QK_DIM), BF16))
    cache_k4 = cache_k.reshape(DEPTH, n_pool, PAGE_SIZE, QK_WIDTH)
    cache_v4 = cache_v.reshape(DEPTH, n_pool, PAGE_SIZE, ATTN_WIDTH)
    pad_rows = (-(BATCH + DEC_BATCH)) % 16
    c_all = jnp.concatenate([c_prompt, c_sample, jnp.zeros((pad_rows, D_MODEL), F32)], axis=0)

    def to_tm(a):
        return a.transpose(1, 0, 2).reshape(ms, a.shape[-1])

    def to_bm(a):
        return a.reshape(DEC_SEQ, DEC_BATCH, a.shape[-1]).transpose(1, 0, 2)

    xp_rows = x_prompt.reshape(mp, D_MODEL)
    xs_rows = to_tm(x_sample)
    outs = [[] for _ in range(10)]
    for l in range(DEPTH):
        lam_init = 0.8 - 0.6 * math.exp(-0.3 * l)
        mod = _ada_call(c_all, w_ada[l], b_ada[l])
        mod_p = mod[0:BATCH].reshape(BATCH, 1, 6 * D_MODEL)
        mod_s = jnp.tile(mod[BATCH:BATCH + DEC_BATCH], (DEC_SEQ, 1)).reshape(1, ms, 6 * D_MODEL)
        gmix = g_mix[l].reshape(1, D_MODEL)
        gmlp = g_mlp[l].reshape(1, D_MODEL)
        w_in_bf = w_in[l].astype(BF16)
        wout_bf = w_out[l].astype(BF16)
        wup_bf = w_up[l].astype(BF16)
        wdown_bf = w_down[l].astype(BF16)
        qg = jnp.tile(q_norm[l], QK_WIDTH // QK_DIM).reshape(1, QK_WIDTH)
        kg = jnp.tile(k_norm[l], QK_WIDTH // QK_DIM).reshape(1, QK_WIDTH)
        mix_wts = (_block_diag(pool_w[l]).astype(BF16), pool_scale[l].reshape(1, -1),
                   conv_w[l], conv_b[l].reshape(1, -1),
                   jnp.concatenate([_block_diag(rg_wa[l]), _block_diag(rg_wx[l])], axis=1).astype(BF16),
                   jnp.concatenate([rg_ba[l], rg_bx[l]]).reshape(1, -1),
                   rg_lambda[l].reshape(1, -1))
        lamp = jnp.stack([lambda_q1[l], lambda_k1[l], lambda_q2[l], lambda_k2[l]])
        sub_g = subln[l].reshape(1, V_DIM)

        xp, xr, gate, q, k, v = _proj_call(xp_rows, mod_p, gmix, w_in_bf, cos_p, sin_p, qg, kg, grp,
                                           tm=PROJ_TM, tiles_per_seq=tm_tiles)
        ypool, yrec, hlast = _mix_call(xp, xr, gate, None, mix_wts, n_seq=BATCH, n_t=SEQ // MIX_TT,
                                       stride=1, rows=MIX_TT, pos0=0)
        o = _attn_prompt_call(q.reshape(BATCH, SEQ, QK_WIDTH), k.reshape(BATCH, SEQ, QK_WIDTH),
                              v.reshape(BATCH, SEQ, ATTN_WIDTH), lamp, sub_g, lam_init=lam_init)
        xp_rows = _out_call(xp_rows, ypool, yrec, o.reshape(mp, ATTN_WIDTH), mod_p, wout_bf, gmlp,
                            wup_bf, wdown_bf, tm=PROJ_TM, tiles_per_seq=tm_tiles)
        outs[0].append(k.reshape(BATCH, SEQ, N_HEADS, 2, QK_DIM))
        outs[1].append(v.reshape(BATCH, SEQ, N_HEADS, V_DIM))
        outs[4].append(xp.reshape(BATCH, SEQ, POOL_WIDTH)[:, SEQ - POOL_HIST:])
        outs[6].append(xr.reshape(BATCH, SEQ, REC_WIDTH)[:, SEQ - (CONV_WIDTH - 1):])
        outs[8].append(hlast.reshape(BATCH, REC_WIDTH))

        xp_s, xr_s, gate_s, q_s, k_s, v_s = _proj_call(xs_rows, mod_s, gmix, w_in_bf, cos_s, sin_s, qg, kg, grp,
                                                        tm=ms, tiles_per_seq=1)
        hist = (state_pool[l].transpose(1, 0, 2).reshape(POOL_HIST * DEC_BATCH, POOL_WIDTH),
                state_conv[l].transpose(1, 0, 2).reshape((CONV_WIDTH - 1) * DEC_BATCH, REC_WIDTH),
                state_rglru[l])
        ypool_s, yrec_s, hlast_s = _mix_call(xp_s, xr_s, gate_s, hist, mix_wts, n_seq=1, n_t=1,
                                             stride=DEC_BATCH, rows=ms, pos0=PAST_LEN)
        k_bm, v_bm = to_bm(k_s), to_bm(v_s)
        o_s = _attn_sample_call(page_table, to_bm(q_s.astype(F32)), k_bm, v_bm, lamp, sub_g,
                                cache_k4, cache_v4, layer=l, lam_init=lam_init)
        xs_rows = _out_call(xs_rows, ypool_s, yrec_s, to_tm(o_s), mod_s, wout_bf, gmlp,
                            wup_bf, wdown_bf, tm=ms, tiles_per_seq=1)
        outs[2].append(k_bm.reshape(DEC_BATCH, DEC_SEQ, N_HEADS, 2, QK_DIM))
        outs[3].append(v_bm.reshape(DEC_BATCH, DEC_SEQ, N_HEADS, V_DIM))
        outs[5].append(jnp.concatenate([state_pool[l][:, DEC_SEQ:], to_bm(xp_s)], axis=1))
        outs[7].append(to_bm(xr_s)[:, DEC_SEQ - (CONV_WIDTH - 1):])
        outs[9].append(hlast_s.reshape(DEC_BATCH, REC_WIDTH))

    return (xp_rows.reshape(BATCH, SEQ, D_MODEL), to_bm(xs_rows)) + tuple(jnp.stack(o) for o in outs)
```

```python
import functools
import math

import jax
import jax.numpy as jnp
from jax import lax
from jax.experimental import pallas as pl
from jax.experimental.pallas import tpu as pltpu

D_MODEL = 1024
BATCH = 4
SEQ = 4096
DEPTH = 2
DEC_BATCH = 32
DEC_SEQ = 4
PAST_LEN = 16384
PAGE_SIZE = 128
N_PAGES = PAST_LEN // PAGE_SIZE

POOL_WIDTH = 256
POOL_WINDOWS = (2, 4, 8, 16)
POOL_GROUP = 64
POOL_HIST = 15
REC_WIDTH = 256
CONV_WIDTH = 4
RGLRU_C = 8.0
N_HEADS = 4
QK_DIM = 64
V_DIM = 128
ATTN_WIDTH = N_HEADS * V_DIM
ROPE_DIM = 16
ROPE_THETA = 500000.0
ATTN_SCALE = 1.0 / math.sqrt(QK_DIM)
Q_PRESCALE = ATTN_SCALE * math.log2(math.e)
QK_WIDTH = N_HEADS * 2 * QK_DIM
IN_WIDTH = POOL_WIDTH + 2 * REC_WIDTH + 2 * QK_WIDTH + ATTN_WIDTH
D_FF = 4 * D_MODEL
EPS = 1e-6

F32 = jnp.float32
BF16 = jnp.bfloat16
NEG_BIG = -1e30
V7X_VMEM_LIMIT = 56 * 1024 * 1024

PROJ_TM = 512
MIX_TT = 512
ATTN_TQ = 1024
ATTN_ROW_CHUNK = 256
PAGES_PER_STEP = 16
PAGE_GROUP = 4
FF_CHUNK = 1024


def _bdot(a, b):
    return jnp.dot(a.astype(BF16), b.astype(BF16), preferred_element_type=F32)


def _params(n_axes):
    return pltpu.CompilerParams(dimension_semantics=("arbitrary",) * n_axes,
                                vmem_limit_bytes=V7X_VMEM_LIMIT)


def _ada_kernel(c_ref, w_ref, b_ref, o_ref):
    c = c_ref[...]
    o_ref[...] = _bdot(c * jax.nn.sigmoid(c), w_ref[...]) + b_ref[...]


def _ada_call(c_all, w_ada, b_ada):
    rows = c_all.shape[0]
    tn = 1536
    return pl.pallas_call(
        _ada_kernel,
        grid=(6 * D_MODEL // tn,),
        in_specs=[pl.BlockSpec((rows, D_MODEL), lambda j: (0, 0)),
                  pl.BlockSpec((D_MODEL, tn), lambda j: (0, j)),
                  pl.BlockSpec((1, tn), lambda j: (0, j))],
        out_specs=pl.BlockSpec((rows, tn), lambda j: (0, j)),
        out_shape=jax.ShapeDtypeStruct((rows, 6 * D_MODEL), F32),
        compiler_params=_params(1),
        name="ada",
    )(c_all, w_ada, b_ada.reshape(1, -1))


def _proj_kernel(x_ref, mod_ref, gmix_ref, w_ref, cos_ref, sin_ref, qg_ref, kg_ref, grp_ref,
                 xp_ref, xr_ref, gate_ref, q_ref, k_ref, v_ref):
    x = x_ref[...]
    y = x * lax.rsqrt(jnp.mean(x * x, axis=-1, keepdims=True) + EPS) * gmix_ref[...]
    sh1 = mod_ref[:, 0:D_MODEL]
    sc1 = mod_ref[:, D_MODEL:2 * D_MODEL]
    h = y * (1.0 + sc1) + sh1
    proj = _bdot(h, w_ref[...])
    xp_ref[...] = proj[:, 0:256]
    xr_ref[...] = proj[:, 256:512]
    gate_ref[...] = proj[:, 512:768]
    v_ref[...] = proj[:, 1792:2304]

    cos = jnp.concatenate([cos_ref[...]] * N_HEADS, axis=1)
    sin = jnp.concatenate([sin_ref[...]] * N_HEADS, axis=1)
    lane = lax.broadcasted_iota(jnp.int32, cos.shape, 1) % QK_DIM
    grp = grp_ref[...]

    def norm_rope(z, g):
        z2 = z * z
        hi = z2.astype(BF16)
        lo = (z2 - hi.astype(F32)).astype(BF16)
        ssq = (jnp.dot(hi, grp, preferred_element_type=F32)
               + jnp.dot(lo, grp, preferred_element_type=F32))
        zn = z * lax.rsqrt(ssq * (1.0 / QK_DIM) + EPS) * g
        half = ROPE_DIM // 2
        swapped = jnp.where(lane < half, pltpu.roll(zn, QK_WIDTH - half, 1), pltpu.roll(zn, half, 1))
        return zn * cos + swapped * sin

    q = norm_rope(proj[:, 768:1280], qg_ref[...])
    q_ref[...] = (q * Q_PRESCALE).astype(q_ref.dtype)
    k_ref[...] = norm_rope(proj[:, 1280:1792], kg_ref[...])


def _proj_call(x, mod3, gmix, w_in_bf, cos_t, sin_t, qg, kg, grp, *, tm, tiles_per_seq):
    m = x.shape[0]
    mod_rows = mod3.shape[1]
    row = lambda w: pl.BlockSpec((tm, w), lambda i: (i, 0))
    full = lambda a: pl.BlockSpec(a.shape, lambda i: (0,) * a.ndim)
    return pl.pallas_call(
        _proj_kernel,
        grid=(m // tm,),
        in_specs=[row(D_MODEL),
                  pl.BlockSpec((None, mod_rows, 6 * D_MODEL), lambda i: (i // tiles_per_seq, 0, 0)),
                  full(gmix), full(w_in_bf),
                  pl.BlockSpec((tm, V_DIM), lambda i: (i % tiles_per_seq, 0)),
                  pl.BlockSpec((tm, V_DIM), lambda i: (i % tiles_per_seq, 0)),
                  full(qg), full(kg), full(grp)],
        out_specs=[row(256), row(256), row(256), row(512), row(512), row(512)],
        out_shape=[jax.ShapeDtypeStruct((m, 256), F32)] * 3
                  + [jax.ShapeDtypeStruct((m, 512), BF16)]
                  + [jax.ShapeDtypeStruct((m, 512), F32)] * 2,
        compiler_params=_params(1),
        name="proj",
    )(x, mod3, gmix, w_in_bf, cos_t, sin_t, qg, kg, grp)


def _mix_kernel(*refs, stride, rows, pos0, has_hist):
    if has_hist:
        (xp_ref, xr_ref, gate_ref, hp_ref, hc_ref, h0_ref, poolw_ref, pscale_ref, convw_ref, convb_ref,
         wax_ref, bax_ref, lam_ref, ypool_ref, yrec_ref, hlast_ref, pext, cext, hcar) = refs
    else:
        (xp_ref, xr_ref, gate_ref, poolw_ref, pscale_ref, convw_ref, convb_ref,
         wax_ref, bax_ref, lam_ref, ypool_ref, yrec_ref, hlast_ref, pext, cext, hcar) = refs
    t = pl.program_id(1)
    tt = rows // stride
    hp_rows = 16 * stride
    hc_rows = 8 * stride

    @pl.when(t == 0)
    def _():
        pext[0:hp_rows, :] = jnp.zeros((hp_rows, POOL_WIDTH), F32)
        cext[0:hc_rows, :] = jnp.zeros((hc_rows, REC_WIDTH), F32)
        if has_hist:
            pext[stride:hp_rows, :] = hp_ref[...]
            cext[(8 - (CONV_WIDTH - 1)) * stride:hc_rows, :] = hc_ref[...]
            hcar[...] = h0_ref[...]
        else:
            hcar[...] = jnp.zeros((stride, REC_WIDTH), F32)

    @pl.when(t > 0)
    def _():
        pext[0:hp_rows, :] = pext[rows:rows + hp_rows, :]
        cext[0:hc_rows, :] = cext[rows:rows + hc_rows, :]

    pext[hp_rows:hp_rows + rows, :] = xp_ref[...]
    cext[hc_rows:hc_rows + rows, :] = xr_ref[...]

    tidx = lax.broadcasted_iota(jnp.int32, (rows, 128), 0) // stride
    lane = lax.broadcasted_iota(jnp.int32, (rows, 128), 1)
    pos1 = (pos0 + 1 + t * tt + tidx).astype(F32)

    def back(j, lo):
        return pext[hp_rows - j * stride:hp_rows - j * stride + rows, lo:lo + 128]

    def window_sums(lo, w_small):
        acc = back(0, lo)
        for j in range(1, w_small):
            acc = acc + back(j, lo)
        big = acc
        for j in range(w_small, 2 * w_small):
            big = big + back(j, lo)
        cnt = jnp.where(lane < POOL_GROUP, jnp.minimum(float(w_small), pos1),
                        jnp.minimum(float(2 * w_small), pos1))
        return jnp.where(lane < POOL_GROUP, acc, big) / cnt - back(0, lo)

    d = jnp.concatenate([window_sums(0, 2), window_sums(128, 8)], axis=1)
    ypool_ref[...] = (_bdot(d, poolw_ref[...]) * pscale_ref[...]).astype(ypool_ref.dtype)

    y = convb_ref[...]
    for tap in range(CONV_WIDTH):
        off = hc_rows - (CONV_WIDTH - 1 - tap) * stride
        y = y + cext[off:off + rows, :] * convw_ref[tap:tap + 1, :]
    rg = _bdot(y, wax_ref[...]) + bax_ref[...]
    r = jax.nn.sigmoid(rg[:, 0:REC_WIDTH])
    i = jax.nn.sigmoid(rg[:, REC_WIDTH:2 * REC_WIDTH])
    nl = -lam_ref[...]
    softplus = jnp.maximum(nl, 0.0) + jnp.log1p(jnp.exp(-jnp.abs(nl)))
    a = jnp.exp(-RGLRU_C * r * softplus)
    b = jnp.sqrt(1.0 - a * a) * (i * y)

    tfull = lax.broadcasted_iota(jnp.int32, (rows, REC_WIDTH), 0) // stride
    step = 1
    while step < tt:
        a_prev = pltpu.roll(a, step * stride, 0)
        b_prev = pltpu.roll(b, step * stride, 0)
        valid = tfull >= step
        b = jnp.where(valid, a * b_prev + b, b)
        a = jnp.where(valid, a * a_prev, a)
        step *= 2
    hc = hcar[...]
    carry = jnp.broadcast_to(hc, (rows, REC_WIDTH)) if stride == 1 else jnp.concatenate([hc] * tt, axis=0)
    h = b + a * carry
    h_end = h[rows - stride:rows, :]
    hcar[...] = h_end
    hlast_ref[...] = h_end

    g = gate_ref[...]
    gelu = 0.5 * g * (1.0 + jnp.tanh(math.sqrt(2.0 / math.pi) * (g + 0.044715 * (g * g * g))))
    yrec_ref[...] = (h * gelu).astype(yrec_ref.dtype)


def _mix_call(xp, xr, gate, hist, wts, *, n_seq, n_t, stride, rows, pos0):
    m = xp.shape[0]
    has_hist = hist is not None
    row = pl.BlockSpec((rows, 256), lambda b, t: (b * n_t + t, 0))
    full = lambda a: pl.BlockSpec(a.shape, lambda b, t: (0,) * a.ndim)
    ins = [xp, xr, gate] + (list(hist) if has_hist else []) + list(wts)
    in_specs = [row, row, row] + [full(a) for a in ins[3:]]
    return pl.pallas_call(
        functools.partial(_mix_kernel, stride=stride, rows=rows, pos0=pos0, has_hist=has_hist),
        grid=(n_seq, n_t),
        in_specs=in_specs,
        out_specs=[row, row, pl.BlockSpec((None, stride, REC_WIDTH), lambda b, t: (b, 0, 0))],
        out_shape=[jax.ShapeDtypeStruct((m, 256), BF16), jax.ShapeDtypeStruct((m, 256), BF16),
                   jax.ShapeDtypeStruct((n_seq, stride, REC_WIDTH), F32)],
        scratch_shapes=[pltpu.VMEM((16 * stride + rows, POOL_WIDTH), F32),
                        pltpu.VMEM((8 * stride + rows, REC_WIDTH), F32),
                        pltpu.VMEM((stride, REC_WIDTH), F32)],
        compiler_params=_params(2),
        name="mix",
    )(*ins)


def _diff_lambda(lamp_ref, lam_init):
    lp = lamp_ref[...]
    s1 = jnp.sum(lp[0:1, :] * lp[1:2, :], axis=-1, keepdims=True)
    s2 = jnp.sum(lp[2:3, :] * lp[3:4, :], axis=-1, keepdims=True)
    return jnp.exp(s1) - jnp.exp(s2) + lam_init


def _subln(o, g, lam_init):
    return o * lax.rsqrt(jnp.mean(o * o, axis=-1, keepdims=True) + EPS) * g * (1.0 - lam_init)


def _attn_prompt_kernel(q_ref, k_ref, v_ref, lamp_ref, subln_ref, o_ref, kb, vb, q2, m_s, acc_s, *,
                        tq, chunk, lam_init):
    qi = pl.program_id(2)

    @pl.when(qi == 0)
    def _():
        kb[...] = k_ref[...].astype(BF16)
        vb[:, 0:V_DIM] = v_ref[...].astype(BF16)
        vb[:, V_DIM:2 * V_DIM] = jnp.ones((SEQ, V_DIM), BF16)

    q = q_ref[...]
    lane = lax.broadcasted_iota(jnp.int32, (tq, 2 * QK_DIM), 1)
    zero = jnp.zeros_like(q)
    q2[0:tq, :] = jnp.where(lane < QK_DIM, q, zero)
    q2[tq:2 * tq, :] = jnp.where(lane >= QK_DIM, q, zero)
    m_s[...] = jnp.full((2 * tq, 128), NEG_BIG, F32)
    acc_s[...] = jnp.zeros((2 * tq, 2 * V_DIM), F32)

    def step(kj, diagonal):
        start = pl.multiple_of(kj * tq, tq)
        for c0 in range(0, 2 * tq, chunk):
            rows = slice(c0, c0 + chunk)
            nk = (c0 % tq) + chunk if diagonal else tq
            s = lax.dot_general(q2[rows, :], kb[pl.ds(start, nk), :], (((1,), (1,)), ((), ())),
                                preferred_element_type=F32)
            if diagonal:
                r = lax.broadcasted_iota(jnp.int32, (chunk, chunk), 0)
                c = lax.broadcasted_iota(jnp.int32, (chunk, chunk), 1)
                tail = jnp.where(c <= r, s[:, nk - chunk:nk], NEG_BIG)
                s = tail if nk == chunk else jnp.concatenate([s[:, 0:nk - chunk], tail], axis=1)
            m_old = m_s[rows, :]
            m_new = jnp.maximum(m_old, jnp.max(s, axis=-1, keepdims=True))
            e = jnp.concatenate([jnp.exp2(s[:, j:j + 128] - m_new) for j in range(0, nk, 128)], axis=1)
            pv = jnp.dot(e.astype(BF16), vb[pl.ds(start, nk), :], preferred_element_type=F32)
            alpha = jnp.exp2(m_old - m_new)
            acc_s[rows, :] = jnp.concatenate([alpha, alpha], axis=1) * acc_s[rows, :] + pv
            m_s[rows, :] = m_new

    def body(kj, carry):
        step(kj, False)
        return carry

    lax.fori_loop(0, qi, body, 0)
    step(qi, True)

    a0 = acc_s[0:tq, :]
    a1 = acc_s[tq:2 * tq, :]
    lam = _diff_lambda(lamp_ref, lam_init)
    o = a0[:, 0:V_DIM] / a0[:, V_DIM:2 * V_DIM] - lam * (a1[:, 0:V_DIM] / a1[:, V_DIM:2 * V_DIM])
    o_ref[...] = _subln(o, subln_ref[...], lam_init).astype(o_ref.dtype)


def _attn_prompt_call(q, k, v, lamp, subln, *, lam_init):
    tq = ATTN_TQ
    qspec = pl.BlockSpec((None, tq, V_DIM), lambda b, h, i: (b, i, h))
    kvspec = pl.BlockSpec((None, SEQ, V_DIM), lambda b, h, i: (b, 0, h))
    full = lambda a: pl.BlockSpec(a.shape, lambda b, h, i: (0,) * a.ndim)
    return pl.pallas_call(
        functools.partial(_attn_prompt_kernel, tq=tq, chunk=ATTN_ROW_CHUNK, lam_init=lam_init),
        grid=(BATCH, N_HEADS, SEQ // tq),
        in_specs=[qspec, kvspec, kvspec, full(lamp), full(subln)],
        out_specs=qspec,
        out_shape=jax.ShapeDtypeStruct((BATCH, SEQ, ATTN_WIDTH), BF16),
        scratch_shapes=[pltpu.VMEM((SEQ, 2 * QK_DIM), BF16),
                        pltpu.VMEM((SEQ, 2 * V_DIM), BF16),
                        pltpu.VMEM((2 * tq, 2 * QK_DIM), BF16),
                        pltpu.VMEM((2 * tq, 128), F32),
                        pltpu.VMEM((2 * tq, 2 * V_DIM), F32)],
        compiler_params=_params(3),
        name="attn_prompt",
    )(q, k, v, lamp, subln)


N_SROWS = DEC_SEQ * N_HEADS * 2


def _attn_sample_kernel(pt_ref, q_ref, kn_ref, vn_ref, lamp_ref, subln_ref, *refs, pages, group, lam_init):
    k_refs = refs[0:pages]
    v_refs = refs[pages:2 * pages]
    o_ref = refs[2 * pages]
    qbd, knp, vnp, m_s, l_s, acc_s = refs[2 * pages + 1:]
    p = pl.program_id(1)
    n_steps = pl.num_programs(1)

    @pl.when(p == 0)
    def _():
        q = q_ref[...]
        r = lax.broadcasted_iota(jnp.int32, (N_SROWS, QK_WIDTH), 0)
        g = lax.broadcasted_iota(jnp.int32, (N_SROWS, QK_WIDTH), 1) // QK_DIM
        rows = jnp.zeros((N_SROWS, QK_WIDTH), F32)
        for t in range(DEC_SEQ):
            rows = jnp.where(r % DEC_SEQ == t, jnp.broadcast_to(q[t:t + 1, :], (N_SROWS, QK_WIDTH)), rows)
        qbd[...] = jnp.where(g == r // DEC_SEQ, rows, 0.0).astype(BF16)
        knp[...] = jnp.zeros((PAGE_SIZE, QK_WIDTH), F32)
        vnp[...] = jnp.zeros((PAGE_SIZE, ATTN_WIDTH), F32)
        knp[0:DEC_SEQ, :] = kn_ref[...]
        vnp[0:DEC_SEQ, :] = vn_ref[...]
        m_s[...] = jnp.full((N_SROWS, 128), NEG_BIG, F32)
        l_s[...] = jnp.zeros((N_SROWS, 128), F32)
        acc_s[...] = jnp.zeros((N_SROWS, V_DIM), F32)

    def accumulate(s_groups, v_groups):
        m_old = m_s[...]
        m_new = m_old
        for s in s_groups:
            m_new = jnp.maximum(m_new, jnp.max(s, axis=-1, keepdims=True))
        alpha = jnp.exp2(m_old - m_new)
        l_new = alpha * l_s[...]
        pv = [None] * N_HEADS
        for s, v_heads in zip(s_groups, v_groups):
            e = jnp.concatenate([jnp.exp2(s[:, j:j + 128] - m_new) for j in range(0, s.shape[1], 128)], axis=1)
            l_new = l_new + jnp.sum(e, axis=-1, keepdims=True)
            eb = e.astype(BF16)
            for h in range(N_HEADS):
                term = jnp.dot(eb[8 * h:8 * h + 8, :], v_heads[h], preferred_element_type=F32)
                pv[h] = term if pv[h] is None else pv[h] + term
        acc_s[...] = alpha * acc_s[...] + jnp.concatenate(pv, axis=0)
        l_s[...] = l_new
        m_s[...] = m_new

    s_groups, v_groups = [], []
    for g0 in range(0, pages, group):
        kt = jnp.concatenate([k_refs[j][...].reshape(QK_WIDTH, PAGE_SIZE).astype(BF16)
                              for j in range(g0, g0 + group)], axis=1)
        s_groups.append(jnp.dot(qbd[...], kt, preferred_element_type=F32))
        v_groups.append([jnp.concatenate([v_refs[j][pl.ds(h, PAGE_SIZE, stride=N_HEADS), :].astype(BF16)
                                          for j in range(g0, g0 + group)], axis=0)
                         for h in range(N_HEADS)])
    accumulate(s_groups, v_groups)

    @pl.when(p == n_steps - 1)
    def _():
        s_new = lax.dot_general(qbd[...], knp[...].astype(BF16), (((1,), (1,)), ((), ())),
                                preferred_element_type=F32)
        row_t = lax.broadcasted_iota(jnp.int32, (N_SROWS, PAGE_SIZE), 0) % DEC_SEQ
        col = lax.broadcasted_iota(jnp.int32, (N_SROWS, PAGE_SIZE), 1)
        vn = vnp[...].astype(BF16)
        accumulate([jnp.where(col <= row_t, s_new, NEG_BIG)],
                   [[vn[:, h * V_DIM:(h + 1) * V_DIM] for h in range(N_HEADS)]])
        on = acc_s[...] / l_s[...]
        lam = _diff_lambda(lamp_ref, lam_init)
        gain = subln_ref[...]
        for h in range(N_HEADS):
            blk = on[8 * h:8 * h + 8, :]
            diff = blk - lam * pltpu.roll(blk, DEC_SEQ, 0)
            o_ref[:, h * V_DIM:(h + 1) * V_DIM] = _subln(diff, gain, lam_init)[0:DEC_SEQ, :]


def _attn_sample_call(page_table, q_bm, k_bm, v_bm, lamp, subln, cache_kt, cache_vr, *, layer, lam_init):
    pages = PAGES_PER_STEP
    n_steps = N_PAGES // pages
    tok = pl.BlockSpec((None, DEC_SEQ, QK_WIDTH), lambda b, p, pt: (b, 0, 0))
    full = lambda a: pl.BlockSpec(a.shape, lambda b, p, pt: (0,) * a.ndim)

    def k_spec(j):
        return pl.BlockSpec((None, None, N_HEADS, 2, QK_DIM, PAGE_SIZE),
                            lambda b, p, pt: (layer, pt[b, p * pages + j], 0, 0, 0, 0))

    def v_spec(j):
        return pl.BlockSpec((None, None, PAGE_SIZE * N_HEADS, V_DIM),
                            lambda b, p, pt: (layer, pt[b, p * pages + j], 0, 0))

    grid_spec = pltpu.PrefetchScalarGridSpec(
        num_scalar_prefetch=1,
        grid=(DEC_BATCH, n_steps),
        in_specs=[tok, tok, tok, full(lamp), full(subln)]
                 + [k_spec(j) for j in range(pages)] + [v_spec(j) for j in range(pages)],
        out_specs=tok,
        scratch_shapes=[pltpu.VMEM((N_SROWS, QK_WIDTH), BF16),
                        pltpu.VMEM((PAGE_SIZE, QK_WIDTH), F32),
                        pltpu.VMEM((PAGE_SIZE, ATTN_WIDTH), F32),
                        pltpu.VMEM((N_SROWS, 128), F32),
                        pltpu.VMEM((N_SROWS, 128), F32),
                        pltpu.VMEM((N_SROWS, V_DIM), F32)])
    return pl.pallas_call(
        functools.partial(_attn_sample_kernel, pages=pages, group=PAGE_GROUP, lam_init=lam_init),
        grid_spec=grid_spec,
        out_shape=jax.ShapeDtypeStruct((DEC_BATCH, DEC_SEQ, ATTN_WIDTH), F32),
        compiler_params=_params(2),
        name="attn_sample",
    )(page_table, q_bm, k_bm, v_bm, lamp, subln, *([cache_kt] * pages), *([cache_vr] * pages))


def _out_kernel(x_ref, yp_ref, yr_ref, o_ref, mod_ref, wout_ref, gmlp_ref, wup_ref, wdown_ref, xo_ref):
    mix = (jnp.dot(yp_ref[...].astype(BF16), wout_ref[0:256, :], preferred_element_type=F32)
           + jnp.dot(yr_ref[...].astype(BF16), wout_ref[256:512, :], preferred_element_type=F32)
           + jnp.dot(o_ref[...].astype(BF16), wout_ref[512:1024, :], preferred_element_type=F32))
    g1 = mod_ref[:, 2 * D_MODEL:3 * D_MODEL]
    sh2 = mod_ref[:, 3 * D_MODEL:4 * D_MODEL]
    sc2 = mod_ref[:, 4 * D_MODEL:5 * D_MODEL]
    g2 = mod_ref[:, 5 * D_MODEL:6 * D_MODEL]
    x1 = x_ref[...] + g1 * mix
    y = x1 * lax.rsqrt(jnp.mean(x1 * x1, axis=-1, keepdims=True) + EPS) * gmlp_ref[...]
    h2 = (y * (1.0 + sc2) + sh2).astype(BF16)
    mlp = None
    for c in range(D_FF // FF_CHUNK):
        up = jnp.dot(h2, wup_ref[:, c * FF_CHUNK:(c + 1) * FF_CHUNK], preferred_element_type=F32)
        act = jnp.square(jnp.maximum(up, 0.0)).astype(BF16)
        term = jnp.dot(act, wdown_ref[c * FF_CHUNK:(c + 1) * FF_CHUNK, :], preferred_element_type=F32)
        mlp = term if mlp is None else mlp + term
    xo_ref[...] = x1 + g2 * mlp


def _out_call(x, yp, yr, o, mod3, wout_bf, gmlp, wup_bf, wdown_bf, *, tm, tiles_per_seq):
    m = x.shape[0]
    mod_rows = mod3.shape[1]
    row = lambda w: pl.BlockSpec((tm, w), lambda i: (i, 0))
    full = lambda a: pl.BlockSpec(a.shape, lambda i: (0,) * a.ndim, pipeline_mode=pl.Buffered(1))
    return pl.pallas_call(
        _out_kernel,
        grid=(m // tm,),
        in_specs=[row(D_MODEL), row(256), row(256), row(512),
                  pl.BlockSpec((None, mod_rows, 6 * D_MODEL), lambda i: (i // tiles_per_seq, 0, 0)),
                  full(wout_bf), full(gmlp), full(wup_bf), full(wdown_bf)],
        out_specs=row(D_MODEL),
        out_shape=jax.ShapeDtypeStruct((m, D_MODEL), F32),
        compiler_params=_params(1),
        name="out",
    )(x, yp, yr, o, mod3, wout_bf, gmlp, wup_bf, wdown_bf)


def _block_diag(w):
    g, c, d = w.shape
    eye = jnp.eye(g, dtype=w.dtype)
    return (eye[:, None, :, None] * w[:, :, None, :]).reshape(g * c, g * d)


def _rope_tables(pos):
    half = ROPE_DIM // 2
    freqs = ROPE_THETA ** (-jnp.arange(half, dtype=F32) / half)
    ang = pos.astype(F32)[:, None] * freqs[None, :]
    cos, sin = jnp.cos(ang), jnp.sin(ang)
    n = pos.shape[0]
    pad1 = jnp.ones((n, QK_DIM - ROPE_DIM), F32)
    pad0 = jnp.zeros((n, QK_DIM - ROPE_DIM), F32)
    c64 = jnp.concatenate([cos, cos, pad1], axis=1)
    s64 = jnp.concatenate([-sin, sin, pad0], axis=1)
    return jnp.concatenate([c64, c64], axis=1), jnp.concatenate([s64, s64], axis=1)


def kernel(x_prompt, x_sample, cache_k, cache_v, page_table, state_pool, state_conv, state_rglru,
           c_prompt, c_sample, w_ada, b_ada, g_mix, w_in, pool_w, pool_scale, conv_w, conv_b,
           rg_wa, rg_ba, rg_wx, rg_bx, rg_lambda, q_norm, k_norm, lambda_q1, lambda_k1,
           lambda_q2, lambda_k2, subln, w_out, g_mlp, w_up, w_down):
    n_pool = cache_k.shape[1]
    mp = BATCH * SEQ
    ms = DEC_BATCH * DEC_SEQ
    tm_tiles = SEQ // PROJ_TM

    cos_p, sin_p = _rope_tables(jnp.arange(SEQ))
    cos_s, sin_s = _rope_tables(PAST_LEN + jnp.repeat(jnp.arange(DEC_SEQ), DEC_BATCH))
    grp = _block_diag(jnp.ones((QK_WIDTH // QK_DIM, QK_DIM, QK_DIM), BF16))
    cache_kt = cache_k.transpose(0, 1, 3, 4, 5, 2)
    cache_vr = cache_v.reshape(DEPTH, n_pool, PAGE_SIZE * N_HEADS, V_DIM)
    pad_rows = (-(BATCH + DEC_BATCH)) % 16
    c_all = jnp.concatenate([c_prompt, c_sample, jnp.zeros((pad_rows, D_MODEL), F32)], axis=0)

    def to_tm(a):
        return a.transpose(1, 0, 2).reshape(ms, a.shape[-1])

    def to_bm(a):
        return a.reshape(DEC_SEQ, DEC_BATCH, a.shape[-1]).transpose(1, 0, 2)

    xp_rows = x_prompt.reshape(mp, D_MODEL)
    xs_rows = to_tm(x_sample)
    outs = [[] for _ in range(10)]
    for l in range(DEPTH):
        lam_init = 0.8 - 0.6 * math.exp(-0.3 * l)
        mod = _ada_call(c_all, w_ada[l], b_ada[l])
        mod_p = mod[0:BATCH].reshape(BATCH, 1, 6 * D_MODEL)
        mod_s = jnp.tile(mod[BATCH:BATCH + DEC_BATCH], (DEC_SEQ, 1)).reshape(1, ms, 6 * D_MODEL)
        gmix = g_mix[l].reshape(1, D_MODEL)
        gmlp = g_mlp[l].reshape(1, D_MODEL)
        w_in_bf = w_in[l].astype(BF16)
        wout_bf = w_out[l].astype(BF16)
        wup_bf = w_up[l].astype(BF16)
        wdown_bf = w_down[l].astype(BF16)
        qg = jnp.tile(q_norm[l], QK_WIDTH // QK_DIM).reshape(1, QK_WIDTH)
        kg = jnp.tile(k_norm[l], QK_WIDTH // QK_DIM).reshape(1, QK_WIDTH)
        mix_wts = (_block_diag(pool_w[l]).astype(BF16), pool_scale[l].reshape(1, -1),
                   conv_w[l], conv_b[l].reshape(1, -1),
                   jnp.concatenate([_block_diag(rg_wa[l]), _block_diag(rg_wx[l])], axis=1).astype(BF16),
                   jnp.concatenate([rg_ba[l], rg_bx[l]]).reshape(1, -1),
                   rg_lambda[l].reshape(1, -1))
        lamp = jnp.stack([lambda_q1[l], lambda_k1[l], lambda_q2[l], lambda_k2[l]])
        sub_g = subln[l].reshape(1, V_DIM)

        xp, xr, gate, q, k, v = _proj_call(xp_rows, mod_p, gmix, w_in_bf, cos_p, sin_p, qg, kg, grp,
                                           tm=PROJ_TM, tiles_per_seq=tm_tiles)
        ypool, yrec, hlast = _mix_call(xp, xr, gate, None, mix_wts, n_seq=BATCH, n_t=SEQ // MIX_TT,
                                       stride=1, rows=MIX_TT, pos0=0)
        o = _attn_prompt_call(q.reshape(BATCH, SEQ, QK_WIDTH), k.reshape(BATCH, SEQ, QK_WIDTH),
                              v.reshape(BATCH, SEQ, ATTN_WIDTH), lamp, sub_g, lam_init=lam_init)
        xp_rows = _out_call(xp_rows, ypool, yrec, o.reshape(mp, ATTN_WIDTH), mod_p, wout_bf, gmlp,
                            wup_bf, wdown_bf, tm=PROJ_TM, tiles_per_seq=tm_tiles)
        outs[0].append(k.reshape(BATCH, SEQ, N_HEADS, 2, QK_DIM))
        outs[1].append(v.reshape(BATCH, SEQ, N_HEADS, V_DIM))
        outs[4].append(xp.reshape(BATCH, SEQ, POOL_WIDTH)[:, SEQ - POOL_HIST:])
        outs[6].append(xr.reshape(BATCH, SEQ, REC_WIDTH)[:, SEQ - (CONV_WIDTH - 1):])
        outs[8].append(hlast.reshape(BATCH, REC_WIDTH))

        xp_s, xr_s, gate_s, q_s, k_s, v_s = _proj_call(xs_rows, mod_s, gmix, w_in_bf, cos_s, sin_s, qg, kg, grp,
                                                        tm=ms, tiles_per_seq=1)
        hist = (state_pool[l].transpose(1, 0, 2).reshape(POOL_HIST * DEC_BATCH, POOL_WIDTH),
                state_conv[l].transpose(1, 0, 2).reshape((CONV_WIDTH - 1) * DEC_BATCH, REC_WIDTH),
                state_rglru[l])
        ypool_s, yrec_s, hlast_s = _mix_call(xp_s, xr_s, gate_s, hist, mix_wts, n_seq=1, n_t=1,
                                             stride=DEC_BATCH, rows=ms, pos0=PAST_LEN)
        k_bm, v_bm = to_bm(k_s), to_bm(v_s)
        o_s = _attn_sample_call(page_table, to_bm(q_s.astype(F32)), k_bm, v_bm, lamp, sub_g,
                                cache_kt, cache_vr, layer=l, lam_init=lam_init)
        xs_rows = _out_call(xs_rows, ypool_s, yrec_s, to_tm(o_s), mod_s, wout_bf, gmlp,
                            wup_bf, wdown_bf, tm=ms, tiles_per_seq=1)
        outs[2].append(k_bm.reshape(DEC_BATCH, DEC_SEQ, N_HEADS, 2, QK_DIM))
        outs[3].append(v_bm.reshape(DEC_BATCH, DEC_SEQ, N_HEADS, V_DIM))
        outs[5].append(jnp.concatenate([state_pool[l][:, DEC_SEQ:], to_bm(xp_s)], axis=1))
        outs[7].append(to_bm(xr_s)[:, DEC_SEQ - (CONV_WIDTH - 1):])
        outs[9].append(hlast_s.reshape(DEC_BATCH, REC_WIDTH))

    return (xp_rows.reshape(BATCH, SEQ, D_MODEL), to_bm(xs_rows)) + tuple(jnp.stack(o) for o in outs)
```

```python
import functools
import math

import jax
import jax.numpy as jnp
import numpy as np
from jax import lax
from jax.experimental import pallas as pl
from jax.experimental.pallas import tpu as pltpu

D_MODEL = 1024
BATCH = 4
SEQ = 4096
DEPTH = 2
DEC_BATCH = 32
DEC_SEQ = 4
PAST_LEN = 16384
PAGE_SIZE = 128
N_PAGES = PAST_LEN // PAGE_SIZE

POOL_WIDTH = 256
POOL_WINDOWS = (2, 4, 8, 16)
POOL_GROUP = 64
POOL_HIST = 15
REC_WIDTH = 256
CONV_WIDTH = 4
RGLRU_C = 8.0
N_HEADS = 4
QK_DIM = 64
V_DIM = 128
ATTN_WIDTH = N_HEADS * V_DIM
ROPE_DIM = 16
ROPE_THETA = 500000.0
ATTN_SCALE = 1.0 / math.sqrt(QK_DIM)
Q_PRESCALE = ATTN_SCALE * math.log2(math.e)
QK_WIDTH = N_HEADS * 2 * QK_DIM
IN_WIDTH = POOL_WIDTH + 2 * REC_WIDTH + 2 * QK_WIDTH + ATTN_WIDTH
D_FF = 4 * D_MODEL
EPS = 1e-6

F32 = jnp.float32
BF16 = jnp.bfloat16
NEG_BIG = -1e30
V7X_VMEM_LIMIT = 56 * 1024 * 1024

PROJ_TM = 512
MIX_TT = 512
ATTN_TQ = 1024
ATTN_ROW_CHUNK = 256
PAGE_GROUP = 8
N_PAIRS = N_PAGES // (2 * PAGE_GROUP)
FF_CHUNK = D_FF // N_PAIRS


def _bdot(a, b):
    return jnp.dot(a.astype(BF16), b.astype(BF16), preferred_element_type=F32)


def _params(n_axes):
    return pltpu.CompilerParams(dimension_semantics=("arbitrary",) * n_axes,
                                vmem_limit_bytes=V7X_VMEM_LIMIT)


def _ada_kernel(c_ref, w_ref, b_ref, o_ref):
    c = c_ref[...]
    o_ref[...] = _bdot(c * jax.nn.sigmoid(c), w_ref[...]) + b_ref[...]


def _ada_call(c_all, w_ada, b_ada, *, layer):
    rows = c_all.shape[0]
    tn = 1536
    return pl.pallas_call(
        _ada_kernel,
        grid=(6 * D_MODEL // tn,),
        in_specs=[pl.BlockSpec((rows, D_MODEL), lambda j: (0, 0)),
                  pl.BlockSpec((None, D_MODEL, tn), lambda j: (layer, 0, j)),
                  pl.BlockSpec((None, 1, tn), lambda j: (layer, 0, j))],
        out_specs=pl.BlockSpec((rows, tn), lambda j: (0, j)),
        out_shape=jax.ShapeDtypeStruct((rows, 6 * D_MODEL), F32),
        compiler_params=_params(1),
        name="ada",
    )(c_all, w_ada, b_ada.reshape(DEPTH, 1, 6 * D_MODEL))


def _proj_kernel(x_ref, mod_ref, gmix_ref, w_ref, cos_ref, sin_ref, qg_ref, kg_ref, grp_ref,
                 xp_ref, xr_ref, gate_ref, q_ref, k_ref, v_ref):
    x = x_ref[...]
    y = x * lax.rsqrt(jnp.mean(x * x, axis=-1, keepdims=True) + EPS) * gmix_ref[...]
    sh1 = mod_ref[:, 0:D_MODEL]
    sc1 = mod_ref[:, D_MODEL:2 * D_MODEL]
    h = y * (1.0 + sc1) + sh1
    proj = _bdot(h, w_ref[...])
    xp_ref[...] = proj[:, 0:256]
    xr_ref[...] = proj[:, 256:512]
    gate_ref[...] = proj[:, 512:768]
    v_ref[...] = proj[:, 1792:2304]

    cos = jnp.concatenate([cos_ref[...]] * N_HEADS, axis=1)
    sin = jnp.concatenate([sin_ref[...]] * N_HEADS, axis=1)
    lane = lax.broadcasted_iota(jnp.int32, cos.shape, 1) % QK_DIM
    grp = grp_ref[...]

    def norm_rope(z, g):
        z2 = z * z
        hi = z2.astype(BF16)
        lo = (z2 - hi.astype(F32)).astype(BF16)
        ssq = (jnp.dot(hi, grp, preferred_element_type=F32)
               + jnp.dot(lo, grp, preferred_element_type=F32))
        zn = z * lax.rsqrt(ssq * (1.0 / QK_DIM) + EPS) * g
        half = ROPE_DIM // 2
        swapped = jnp.where(lane < half, pltpu.roll(zn, QK_WIDTH - half, 1), pltpu.roll(zn, half, 1))
        return zn * cos + swapped * sin

    q = norm_rope(proj[:, 768:1280], qg_ref[...])
    q_ref[...] = (q * Q_PRESCALE).astype(q_ref.dtype)
    k_ref[...] = norm_rope(proj[:, 1280:1792], kg_ref[...])


def _proj_call(x, mod3, gmix, w_in_bf, cos_t, sin_t, qg, kg, grp, *, tm, tiles_per_seq):
    m = x.shape[0]
    mod_rows = mod3.shape[1]
    row = lambda w: pl.BlockSpec((tm, w), lambda i: (i, 0))
    full = lambda a: pl.BlockSpec(a.shape, lambda i: (0,) * a.ndim)
    return pl.pallas_call(
        _proj_kernel,
        grid=(m // tm,),
        in_specs=[row(D_MODEL),
                  pl.BlockSpec((None, mod_rows, 6 * D_MODEL), lambda i: (i // tiles_per_seq, 0, 0)),
                  full(gmix), full(w_in_bf),
                  pl.BlockSpec((tm, V_DIM), lambda i: (i % tiles_per_seq, 0)),
                  pl.BlockSpec((tm, V_DIM), lambda i: (i % tiles_per_seq, 0)),
                  full(qg), full(kg), full(grp)],
        out_specs=[row(256), row(256), row(256), row(512), row(512), row(512)],
        out_shape=[jax.ShapeDtypeStruct((m, 256), F32)] * 3
                  + [jax.ShapeDtypeStruct((m, 512), BF16)]
                  + [jax.ShapeDtypeStruct((m, 512), F32)] * 2,
        compiler_params=_params(1),
        name="proj",
    )(x, mod3, gmix, w_in_bf, cos_t, sin_t, qg, kg, grp)


def _mix_kernel(*refs, stride, rows, pos0, has_hist):
    if has_hist:
        (xp_ref, xr_ref, gate_ref, hp_ref, hc_ref, h0_ref, poolw_ref, pscale_ref, convw_ref, convb_ref,
         wax_ref, bax_ref, lam_ref, ypool_ref, yrec_ref, hlast_ref, pext, cext, hcar) = refs
    else:
        (xp_ref, xr_ref, gate_ref, poolw_ref, pscale_ref, convw_ref, convb_ref,
         wax_ref, bax_ref, lam_ref, ypool_ref, yrec_ref, hlast_ref, pext, cext, hcar) = refs
    t = pl.program_id(1)
    tt = rows // stride
    hp_rows = 16 * stride
    hc_rows = 8 * stride

    @pl.when(t == 0)
    def _():
        pext[0:hp_rows, :] = jnp.zeros((hp_rows, POOL_WIDTH), F32)
        cext[0:hc_rows, :] = jnp.zeros((hc_rows, REC_WIDTH), F32)
        if has_hist:
            pext[stride:hp_rows, :] = hp_ref[...]
            cext[(8 - (CONV_WIDTH - 1)) * stride:hc_rows, :] = hc_ref[...]
            hcar[...] = h0_ref[...]
        else:
            hcar[...] = jnp.zeros((stride, REC_WIDTH), F32)

    @pl.when(t > 0)
    def _():
        pext[0:hp_rows, :] = pext[rows:rows + hp_rows, :]
        cext[0:hc_rows, :] = cext[rows:rows + hc_rows, :]

    pext[hp_rows:hp_rows + rows, :] = xp_ref[...]
    cext[hc_rows:hc_rows + rows, :] = xr_ref[...]

    tidx = lax.broadcasted_iota(jnp.int32, (rows, 128), 0) // stride
    lane = lax.broadcasted_iota(jnp.int32, (rows, 128), 1)
    pos1 = (pos0 + 1 + t * tt + tidx).astype(F32)

    def back(j, lo):
        return pext[hp_rows - j * stride:hp_rows - j * stride + rows, lo:lo + 128]

    def window_sums(lo, w_small):
        acc = back(0, lo)
        for j in range(1, w_small):
            acc = acc + back(j, lo)
        big = acc
        for j in range(w_small, 2 * w_small):
            big = big + back(j, lo)
        cnt = jnp.where(lane < POOL_GROUP, jnp.minimum(float(w_small), pos1),
                        jnp.minimum(float(2 * w_small), pos1))
        return jnp.where(lane < POOL_GROUP, acc, big) / cnt - back(0, lo)

    d = jnp.concatenate([window_sums(0, 2), window_sums(128, 8)], axis=1)
    ypool_ref[...] = (_bdot(d, poolw_ref[...]) * pscale_ref[...]).astype(ypool_ref.dtype)

    y = convb_ref[...]
    for tap in range(CONV_WIDTH):
        off = hc_rows - (CONV_WIDTH - 1 - tap) * stride
        y = y + cext[off:off + rows, :] * convw_ref[tap:tap + 1, :]
    rg = _bdot(y, wax_ref[...]) + bax_ref[...]
    r = jax.nn.sigmoid(rg[:, 0:REC_WIDTH])
    i = jax.nn.sigmoid(rg[:, REC_WIDTH:2 * REC_WIDTH])
    nl = -lam_ref[...]
    softplus = jnp.maximum(nl, 0.0) + jnp.log1p(jnp.exp(-jnp.abs(nl)))
    a = jnp.exp(-RGLRU_C * r * softplus)
    b = jnp.sqrt(1.0 - a * a) * (i * y)

    tfull = lax.broadcasted_iota(jnp.int32, (rows, REC_WIDTH), 0) // stride
    step = 1
    while step < tt:
        a_prev = pltpu.roll(a, step * stride, 0)
        b_prev = pltpu.roll(b, step * stride, 0)
        valid = tfull >= step
        b = jnp.where(valid, a * b_prev + b, b)
        a = jnp.where(valid, a * a_prev, a)
        step *= 2
    hc = hcar[...]
    carry = jnp.broadcast_to(hc, (rows, REC_WIDTH)) if stride == 1 else jnp.concatenate([hc] * tt, axis=0)
    h = b + a * carry
    h_end = h[rows - stride:rows, :]
    hcar[...] = h_end
    hlast_ref[...] = h_end

    g = gate_ref[...]
    gelu = 0.5 * g * (1.0 + jnp.tanh(math.sqrt(2.0 / math.pi) * (g + 0.044715 * (g * g * g))))
    yrec_ref[...] = (h * gelu).astype(yrec_ref.dtype)


def _mix_call(xp, xr, gate, hist, wts, *, n_seq, n_t, stride, rows, pos0):
    m = xp.shape[0]
    has_hist = hist is not None
    row = pl.BlockSpec((rows, 256), lambda b, t: (b * n_t + t, 0))
    full = lambda a: pl.BlockSpec(a.shape, lambda b, t: (0,) * a.ndim)
    ins = [xp, xr, gate] + (list(hist) if has_hist else []) + list(wts)
    in_specs = [row, row, row] + [full(a) for a in ins[3:]]
    return pl.pallas_call(
        functools.partial(_mix_kernel, stride=stride, rows=rows, pos0=pos0, has_hist=has_hist),
        grid=(n_seq, n_t),
        in_specs=in_specs,
        out_specs=[row, row, pl.BlockSpec((None, stride, REC_WIDTH), lambda b, t: (b, 0, 0))],
        out_shape=[jax.ShapeDtypeStruct((m, 256), BF16), jax.ShapeDtypeStruct((m, 256), BF16),
                   jax.ShapeDtypeStruct((n_seq, stride, REC_WIDTH), F32)],
        scratch_shapes=[pltpu.VMEM((16 * stride + rows, POOL_WIDTH), F32),
                        pltpu.VMEM((8 * stride + rows, REC_WIDTH), F32),
                        pltpu.VMEM((stride, REC_WIDTH), F32)],
        compiler_params=_params(2),
        name="mix",
    )(*ins)


def _diff_lambda(lamp_ref, lam_init):
    lp = lamp_ref[...]
    s1 = jnp.sum(lp[0:1, :] * lp[1:2, :], axis=-1, keepdims=True)
    s2 = jnp.sum(lp[2:3, :] * lp[3:4, :], axis=-1, keepdims=True)
    return jnp.exp(s1) - jnp.exp(s2) + lam_init


def _subln(o, g, lam_init):
    return o * lax.rsqrt(jnp.mean(o * o, axis=-1, keepdims=True) + EPS) * g * (1.0 - lam_init)


def _attn_prompt_kernel(q_ref, k_ref, v_ref, lamp_ref, subln_ref, o_ref, kb, vb, q2, m_s, acc_s, *,
                        tq, chunk, lam_init):
    qi = pl.program_id(2)

    @pl.when(qi == 0)
    def _():
        kb[...] = k_ref[...].astype(BF16)
        vb[:, 0:V_DIM] = v_ref[...].astype(BF16)
        vb[:, V_DIM:2 * V_DIM] = jnp.ones((SEQ, V_DIM), BF16)

    q = q_ref[...]
    lane = lax.broadcasted_iota(jnp.int32, (tq, 2 * QK_DIM), 1)
    zero = jnp.zeros_like(q)
    q2[0:tq, :] = jnp.where(lane < QK_DIM, q, zero)
    q2[tq:2 * tq, :] = jnp.where(lane >= QK_DIM, q, zero)
    m_s[...] = jnp.full((2 * tq, 128), NEG_BIG, F32)
    acc_s[...] = jnp.zeros((2 * tq, 2 * V_DIM), F32)

    def step(kj, diagonal):
        start = pl.multiple_of(kj * tq, tq)
        for c0 in range(0, 2 * tq, chunk):
            rows = slice(c0, c0 + chunk)
            nk = (c0 % tq) + chunk if diagonal else tq
            s = lax.dot_general(q2[rows, :], kb[pl.ds(start, nk), :], (((1,), (1,)), ((), ())),
                                preferred_element_type=F32)
            if diagonal:
                r = lax.broadcasted_iota(jnp.int32, (chunk, chunk), 0)
                c = lax.broadcasted_iota(jnp.int32, (chunk, chunk), 1)
                tail = jnp.where(c <= r, s[:, nk - chunk:nk], NEG_BIG)
                s = tail if nk == chunk else jnp.concatenate([s[:, 0:nk - chunk], tail], axis=1)
            m_old = m_s[rows, :]
            m_new = jnp.maximum(m_old, jnp.max(s, axis=-1, keepdims=True))
            e = jnp.concatenate([jnp.exp2(s[:, j:j + 128] - m_new) for j in range(0, nk, 128)], axis=1)
            pv = jnp.dot(e.astype(BF16), vb[pl.ds(start, nk), :], preferred_element_type=F32)
            alpha = jnp.exp2(m_old - m_new)
            acc_s[rows, :] = jnp.concatenate([alpha, alpha], axis=1) * acc_s[rows, :] + pv
            m_s[rows, :] = m_new

    def body(kj, carry):
        step(kj, False)
        return carry

    lax.fori_loop(0, qi, body, 0)
    step(qi, True)

    a0 = acc_s[0:tq, :]
    a1 = acc_s[tq:2 * tq, :]
    lam = _diff_lambda(lamp_ref, lam_init)
    o = a0[:, 0:V_DIM] / a0[:, V_DIM:2 * V_DIM] - lam * (a1[:, 0:V_DIM] / a1[:, V_DIM:2 * V_DIM])
    o_ref[...] = _subln(o, subln_ref[...], lam_init).astype(o_ref.dtype)


def _attn_prompt_call(q, k, v, lamp, subln, *, lam_init):
    tq = ATTN_TQ
    qspec = pl.BlockSpec((None, tq, V_DIM), lambda b, h, i: (b, i, h))
    kvspec = pl.BlockSpec((None, SEQ, V_DIM), lambda b, h, i: (b, 0, h))
    full = lambda a: pl.BlockSpec(a.shape, lambda b, h, i: (0,) * a.ndim)
    return pl.pallas_call(
        functools.partial(_attn_prompt_kernel, tq=tq, chunk=ATTN_ROW_CHUNK, lam_init=lam_init),
        grid=(BATCH, N_HEADS, SEQ // tq),
        in_specs=[qspec, kvspec, kvspec, full(lamp), full(subln)],
        out_specs=qspec,
        out_shape=jax.ShapeDtypeStruct((BATCH, SEQ, ATTN_WIDTH), BF16),
        scratch_shapes=[pltpu.VMEM((SEQ, 2 * QK_DIM), BF16),
                        pltpu.VMEM((SEQ, 2 * V_DIM), BF16),
                        pltpu.VMEM((2 * tq, 2 * QK_DIM), BF16),
                        pltpu.VMEM((2 * tq, 128), F32),
                        pltpu.VMEM((2 * tq, 2 * V_DIM), F32)],
        compiler_params=_params(3),
        name="attn_prompt",
    )(q, k, v, lamp, subln)


N_SROWS = DEC_SEQ * N_HEADS * 2


def _sample_query_rows(q):
    r = lax.broadcasted_iota(jnp.int32, (N_SROWS, QK_WIDTH), 0)
    g = lax.broadcasted_iota(jnp.int32, (N_SROWS, QK_WIDTH), 1) // QK_DIM
    rows = jnp.zeros((N_SROWS, QK_WIDTH), F32)
    for t in range(DEC_SEQ):
        rows = jnp.where(r % DEC_SEQ == t, jnp.broadcast_to(q[t:t + 1, :], (N_SROWS, QK_WIDTH)), rows)
    return jnp.where(g == r // DEC_SEQ, rows, 0.0).astype(BF16)


def _sample_softmax_update(state, s, v_heads):
    m_old, l_old, acc_old = state
    m_new = jnp.maximum(m_old, jnp.max(s, axis=-1, keepdims=True))
    alpha = jnp.exp2(m_old - m_new)
    e = jnp.concatenate([jnp.exp2(s[:, j:j + 128] - m_new) for j in range(0, s.shape[1], 128)], axis=1)
    l_new = alpha * l_old + jnp.sum(e, axis=-1, keepdims=True)
    eb = e.astype(BF16)
    pv = [jnp.dot(eb[8 * h:8 * h + 8, :], v_heads[h], preferred_element_type=F32) for h in range(N_HEADS)]
    return m_new, l_new, alpha * acc_old + jnp.concatenate(pv, axis=0)


def _sample_new_rows_and_finish(state, qbd, kn_ref, vn_ref, knp_ref, vnp_ref, lamp_ref, subln_ref, os_ref, lam_init):
    knp_ref[...] = jnp.zeros((PAGE_SIZE, QK_WIDTH), F32)
    vnp_ref[...] = jnp.zeros((PAGE_SIZE, ATTN_WIDTH), F32)
    knp_ref[0:DEC_SEQ, :] = kn_ref[...]
    vnp_ref[0:DEC_SEQ, :] = vn_ref[...]
    knp = knp_ref[...].astype(BF16)
    vnp = vnp_ref[...].astype(BF16)
    s_new = lax.dot_general(qbd, knp, (((1,), (1,)), ((), ())), preferred_element_type=F32)
    row_t = lax.broadcasted_iota(jnp.int32, (N_SROWS, PAGE_SIZE), 0) % DEC_SEQ
    col = lax.broadcasted_iota(jnp.int32, (N_SROWS, PAGE_SIZE), 1)
    state = _sample_softmax_update(state, jnp.where(col <= row_t, s_new, NEG_BIG),
                                   [vnp[:, h * V_DIM:(h + 1) * V_DIM] for h in range(N_HEADS)])
    _, l, acc = state
    on = acc / l
    lam = _diff_lambda(lamp_ref, lam_init)
    gain = subln_ref[...]
    for h in range(N_HEADS):
        blk = on[8 * h:8 * h + 8, :]
        diff = blk - lam * pltpu.roll(blk, DEC_SEQ, 0)
        os_ref[:, h * V_DIM:(h + 1) * V_DIM] = _subln(diff, gain, lam_init)[0:DEC_SEQ, :]


def _residual_and_mlp_input(x_ref, yp_ref, yr_ref, o_ref, mod_ref, wout_ref, gmlp_ref):
    mix = (jnp.dot(yp_ref[...].astype(BF16), wout_ref[0:256, :], preferred_element_type=F32)
           + jnp.dot(yr_ref[...].astype(BF16), wout_ref[256:512, :], preferred_element_type=F32)
           + jnp.dot(o_ref[...].astype(BF16), wout_ref[512:1024, :], preferred_element_type=F32))
    g1 = mod_ref[:, 2 * D_MODEL:3 * D_MODEL]
    sh2 = mod_ref[:, 3 * D_MODEL:4 * D_MODEL]
    sc2 = mod_ref[:, 4 * D_MODEL:5 * D_MODEL]
    x1 = x_ref[...] + g1 * mix
    y = x1 * lax.rsqrt(jnp.mean(x1 * x1, axis=-1, keepdims=True) + EPS) * gmlp_ref[...]
    return x1, (y * (1.0 + sc2) + sh2).astype(BF16)


def _mlp_chunk(h2, wup_ref, wdown_ref, c):
    up = jnp.dot(h2, wup_ref[c], preferred_element_type=F32)
    act = jnp.square(jnp.maximum(up, 0.0)).astype(BF16)
    return jnp.dot(act, wdown_ref[c], preferred_element_type=F32)


def _out_kernel(x_ref, yp_ref, yr_ref, o_ref, mod_ref, wout_ref, gmlp_ref, wup_ref, wdown_ref, xo_ref):
    x1, h2 = _residual_and_mlp_input(x_ref, yp_ref, yr_ref, o_ref, mod_ref, wout_ref, gmlp_ref)
    mlp = _mlp_chunk(h2, wup_ref, wdown_ref, 0)
    for c in range(1, N_PAIRS):
        mlp = mlp + _mlp_chunk(h2, wup_ref, wdown_ref, c)
    xo_ref[...] = x1 + mod_ref[:, 5 * D_MODEL:6 * D_MODEL] * mlp


def _out_call(x, yp, yr, o, mod3, wout_bf, gmlp, wup3, wdown3, *, tm, tiles_per_seq):
    m = x.shape[0]
    mod_rows = mod3.shape[1]
    row = lambda w: pl.BlockSpec((tm, w), lambda i: (i, 0))
    full = lambda a: pl.BlockSpec(a.shape, lambda i: (0,) * a.ndim, pipeline_mode=pl.Buffered(1))
    return pl.pallas_call(
        _out_kernel,
        grid=(m // tm,),
        in_specs=[row(D_MODEL), row(256), row(256), row(512),
                  pl.BlockSpec((None, mod_rows, 6 * D_MODEL), lambda i: (i // tiles_per_seq, 0, 0)),
                  full(wout_bf), full(gmlp), full(wup3), full(wdown3)],
        out_specs=row(D_MODEL),
        out_shape=jax.ShapeDtypeStruct((m, D_MODEL), F32),
        compiler_params=_params(1),
        name="out",
    )(x, yp, yr, o, mod3, wout_bf, gmlp, wup3, wdown3)


def _out_attn_kernel(pt_ref, x_ref, yp_ref, yr_ref, o_ref, mod_ref, wout_ref, gmlp_ref, wup_ref, wdown_ref,
                     q_ref, kn_ref, vn_ref, lamp_ref, subln_ref, ck_ref, cv_ref,
                     xo_ref, os_ref,
                     kbuf, vbuf, sem, h2_s, mlp_s, qbd_s, knp_s, vnp_s, *, layer, lam_init):
    i = pl.program_id(0)
    n_steps = pl.num_programs(0)

    def page_copies(seq, group, slot):
        copies = []
        for j in range(PAGE_GROUP):
            page = pt_ref[seq, group * PAGE_GROUP + j]
            copies.append(pltpu.make_async_copy(ck_ref.at[layer, page], kbuf.at[slot, j], sem.at[0, slot]))
            copies.append(pltpu.make_async_copy(cv_ref.at[layer, page], vbuf.at[slot, j], sem.at[1, slot]))
        return copies

    @pl.when(i == 0)
    def _():
        for slot in range(2):
            for c in page_copies(0, slot, slot):
                c.start()

    x1, h2 = _residual_and_mlp_input(x_ref, yp_ref, yr_ref, o_ref, mod_ref, wout_ref, gmlp_ref)
    xo_ref[...] = x1
    h2_s[...] = h2
    mlp_s[...] = jnp.zeros(mlp_s.shape, F32)
    qbd_s[...] = _sample_query_rows(q_ref[...])

    def page_group(slot, state):
        kt = jnp.concatenate([kbuf[slot, j].reshape(QK_WIDTH, PAGE_SIZE).astype(BF16)
                              for j in range(PAGE_GROUP)], axis=1)
        s = jnp.dot(qbd_s[...], kt, preferred_element_type=F32)
        v_heads = [jnp.concatenate([vbuf[slot, j, pl.ds(h, PAGE_SIZE, stride=N_HEADS), :].astype(BF16)
                                    for j in range(PAGE_GROUP)], axis=0) for h in range(N_HEADS)]
        return _sample_softmax_update(state, s, v_heads)

    def pair(p, state):
        last = p == N_PAIRS - 1
        for slot in range(2):
            group = 2 * p + slot
            for c in page_copies(i, group, slot):
                c.wait()
            state = page_group(slot, state)
            nxt_seq = jnp.where(last, i + 1, i)
            nxt_group = jnp.where(last, slot, group + 2)

            @pl.when(jnp.logical_or(jnp.logical_not(last), i + 1 < n_steps))
            def _():
                for c in page_copies(nxt_seq, nxt_group, slot):
                    c.start()
        mlp_s[...] += _mlp_chunk(h2_s[...], wup_ref, wdown_ref, p)
        return state

    state0 = (jnp.full((N_SROWS, 128), NEG_BIG, F32), jnp.zeros((N_SROWS, 128), F32),
              jnp.zeros((N_SROWS, V_DIM), F32))
    state = lax.fori_loop(0, N_PAIRS, pair, state0)

    _sample_new_rows_and_finish(state, qbd_s[...], kn_ref, vn_ref, knp_s, vnp_s, lamp_ref, subln_ref, os_ref, lam_init)
    xo_ref[...] = xo_ref[...] + mod_ref[:, 5 * D_MODEL:6 * D_MODEL] * mlp_s[...]


def _out_attn_call(page_table, x, yp, yr, o, mod3, wout_bf, gmlp, wup3, wdown3,
                   q_bm, k_bm, v_bm, lamp, subln, cache_kt, cache_vr, *, tm, tiles_per_seq, layer, lam_init):
    m = x.shape[0]
    assert m // tm == DEC_BATCH, "one prompt row tile per sample sequence"
    row = lambda w: pl.BlockSpec((tm, w), lambda i, pt: (i, 0))
    full = lambda a: pl.BlockSpec(a.shape, lambda i, pt: (0,) * a.ndim, pipeline_mode=pl.Buffered(1))
    tok = pl.BlockSpec((None, DEC_SEQ, QK_WIDTH), lambda i, pt: (i, 0, 0))
    hbm = pl.BlockSpec(memory_space=pl.ANY)
    grid_spec = pltpu.PrefetchScalarGridSpec(
        num_scalar_prefetch=1,
        grid=(m // tm,),
        in_specs=[row(D_MODEL), row(256), row(256), row(512),
                  pl.BlockSpec((None, 1, 6 * D_MODEL), lambda i, pt: (i // tiles_per_seq, 0, 0)),
                  full(wout_bf), full(gmlp), full(wup3), full(wdown3),
                  tok, tok, tok, full(lamp), full(subln), hbm, hbm],
        out_specs=[row(D_MODEL), tok],
        scratch_shapes=[pltpu.VMEM((2, PAGE_GROUP, N_HEADS, 2, QK_DIM, PAGE_SIZE), F32),
                        pltpu.VMEM((2, PAGE_GROUP, PAGE_SIZE * N_HEADS, V_DIM), F32),
                        pltpu.SemaphoreType.DMA((2, 2)),
                        pltpu.VMEM((tm, D_MODEL), BF16),
                        pltpu.VMEM((tm, D_MODEL), F32),
                        pltpu.VMEM((N_SROWS, QK_WIDTH), BF16),
                        pltpu.VMEM((PAGE_SIZE, QK_WIDTH), F32),
                        pltpu.VMEM((PAGE_SIZE, ATTN_WIDTH), F32)])
    return pl.pallas_call(
        functools.partial(_out_attn_kernel, layer=layer, lam_init=lam_init),
        grid_spec=grid_spec,
        out_shape=[jax.ShapeDtypeStruct((m, D_MODEL), F32),
                   jax.ShapeDtypeStruct((DEC_BATCH, DEC_SEQ, ATTN_WIDTH), F32)],
        compiler_params=_params(1),
        name="out_attn",
    )(page_table, x, yp, yr, o, mod3, wout_bf, gmlp, wup3, wdown3, q_bm, k_bm, v_bm, lamp, subln,
      cache_kt, cache_vr)


def _block_diag(w):
    g, c, d = w.shape
    eye = jnp.eye(g, dtype=w.dtype)
    return (eye[:, None, :, None] * w[:, :, None, :]).reshape(g * c, g * d)


def _rope_tables(pos):
    half = ROPE_DIM // 2
    freqs = ROPE_THETA ** (-np.arange(half, dtype=np.float64) / half)
    ang = np.asarray(pos, np.float64)[:, None] * freqs[None, :]
    cos, sin = np.cos(ang), np.sin(ang)
    n = ang.shape[0]
    c64 = np.concatenate([cos, cos, np.ones((n, QK_DIM - ROPE_DIM))], axis=1)
    s64 = np.concatenate([-sin, sin, np.zeros((n, QK_DIM - ROPE_DIM))], axis=1)
    return (jnp.asarray(np.concatenate([c64, c64], axis=1), F32),
            jnp.asarray(np.concatenate([s64, s64], axis=1), F32))


def kernel(x_prompt, x_sample, cache_k, cache_v, page_table, state_pool, state_conv, state_rglru,
           c_prompt, c_sample, w_ada, b_ada, g_mix, w_in, pool_w, pool_scale, conv_w, conv_b,
           rg_wa, rg_ba, rg_wx, rg_bx, rg_lambda, q_norm, k_norm, lambda_q1, lambda_k1,
           lambda_q2, lambda_k2, subln, w_out, g_mlp, w_up, w_down):
    n_pool = cache_k.shape[1]
    mp = BATCH * SEQ
    ms = DEC_BATCH * DEC_SEQ
    tm_tiles = SEQ // PROJ_TM

    cos_p, sin_p = _rope_tables(np.arange(SEQ))
    cos_s, sin_s = _rope_tables(PAST_LEN + np.repeat(np.arange(DEC_SEQ), DEC_BATCH))
    grp = _block_diag(jnp.ones((QK_WIDTH // QK_DIM, QK_DIM, QK_DIM), BF16))
    cache_kt = cache_k.transpose(0, 1, 3, 4, 5, 2)
    cache_vr = cache_v.reshape(DEPTH, n_pool, PAGE_SIZE * N_HEADS, V_DIM)
    pad_rows = (-(BATCH + DEC_BATCH)) % 16
    c_all = jnp.concatenate([c_prompt, c_sample, jnp.zeros((pad_rows, D_MODEL), F32)], axis=0)

    def to_tm(a):
        return a.transpose(1, 0, 2).reshape(ms, a.shape[-1])

    def to_bm(a):
        return a.reshape(DEC_SEQ, DEC_BATCH, a.shape[-1]).transpose(1, 0, 2)

    xp_rows = x_prompt.reshape(mp, D_MODEL)
    xs_rows = to_tm(x_sample)
    outs = [[] for _ in range(10)]
    for l in range(DEPTH):
        lam_init = 0.8 - 0.6 * math.exp(-0.3 * l)
        mod = _ada_call(c_all, w_ada, b_ada, layer=l)
        mod_p = mod[0:BATCH].reshape(BATCH, 1, 6 * D_MODEL)
        mod_s = jnp.tile(mod[BATCH:BATCH + DEC_BATCH], (DEC_SEQ, 1)).reshape(1, ms, 6 * D_MODEL)
        gmix = g_mix[l].reshape(1, D_MODEL)
        gmlp = g_mlp[l].reshape(1, D_MODEL)
        w_in_bf = w_in[l].astype(BF16)
        wout_bf = w_out[l].astype(BF16)
        wup3 = w_up[l].astype(BF16).reshape(D_MODEL, N_PAIRS, FF_CHUNK).transpose(1, 0, 2)
        wdown3 = w_down[l].astype(BF16).reshape(N_PAIRS, FF_CHUNK, D_MODEL)
        qg = jnp.tile(q_norm[l], QK_WIDTH // QK_DIM).reshape(1, QK_WIDTH)
        kg = jnp.tile(k_norm[l], QK_WIDTH // QK_DIM).reshape(1, QK_WIDTH)
        mix_wts = (_block_diag(pool_w[l]).astype(BF16), pool_scale[l].reshape(1, -1),
                   conv_w[l], conv_b[l].reshape(1, -1),
                   jnp.concatenate([_block_diag(rg_wa[l]), _block_diag(rg_wx[l])], axis=1).astype(BF16),
                   jnp.concatenate([rg_ba[l], rg_bx[l]]).reshape(1, -1),
                   rg_lambda[l].reshape(1, -1))
        lamp = jnp.stack([lambda_q1[l], lambda_k1[l], lambda_q2[l], lambda_k2[l]])
        sub_g = subln[l].reshape(1, V_DIM)

        xp, xr, gate, q, k, v = _proj_call(xp_rows, mod_p, gmix, w_in_bf, cos_p, sin_p, qg, kg, grp,
                                           tm=PROJ_TM, tiles_per_seq=tm_tiles)
        ypool, yrec, hlast = _mix_call(xp, xr, gate, None, mix_wts, n_seq=BATCH, n_t=SEQ // MIX_TT,
                                       stride=1, rows=MIX_TT, pos0=0)
        o = _attn_prompt_call(q.reshape(BATCH, SEQ, QK_WIDTH), k.reshape(BATCH, SEQ, QK_WIDTH),
                              v.reshape(BATCH, SEQ, ATTN_WIDTH), lamp, sub_g, lam_init=lam_init)
        outs[0].append(k.reshape(BATCH, SEQ, N_HEADS, 2, QK_DIM))
        outs[1].append(v.reshape(BATCH, SEQ, N_HEADS, V_DIM))
        outs[4].append(xp.reshape(BATCH, SEQ, POOL_WIDTH)[:, SEQ - POOL_HIST:])
        outs[6].append(xr.reshape(BATCH, SEQ, REC_WIDTH)[:, SEQ - (CONV_WIDTH - 1):])
        outs[8].append(hlast.reshape(BATCH, REC_WIDTH))

        xp_s, xr_s, gate_s, q_s, k_s, v_s = _proj_call(xs_rows, mod_s, gmix, w_in_bf, cos_s, sin_s, qg, kg, grp,
                                                        tm=ms, tiles_per_seq=1)
        hist = (state_pool[l].transpose(1, 0, 2).reshape(POOL_HIST * DEC_BATCH, POOL_WIDTH),
                state_conv[l].transpose(1, 0, 2).reshape((CONV_WIDTH - 1) * DEC_BATCH, REC_WIDTH),
                state_rglru[l])
        ypool_s, yrec_s, hlast_s = _mix_call(xp_s, xr_s, gate_s, hist, mix_wts, n_seq=1, n_t=1,
                                             stride=DEC_BATCH, rows=ms, pos0=PAST_LEN)
        k_bm, v_bm = to_bm(k_s), to_bm(v_s)

        xp_rows, o_s = _out_attn_call(page_table, xp_rows, ypool, yrec, o.reshape(mp, ATTN_WIDTH), mod_p, wout_bf,
                                      gmlp, wup3, wdown3, to_bm(q_s.astype(F32)), k_bm, v_bm, lamp, sub_g,
                                      cache_kt, cache_vr, tm=PROJ_TM, tiles_per_seq=tm_tiles, layer=l,
                                      lam_init=lam_init)
        xs_rows = _out_call(xs_rows, ypool_s, yrec_s, to_tm(o_s), mod_s, wout_bf, gmlp,
                            wup3, wdown3, tm=ms, tiles_per_seq=1)
        outs[2].append(k_bm.reshape(DEC_BATCH, DEC_SEQ, N_HEADS, 2, QK_DIM))
        outs[3].append(v_bm.reshape(DEC_BATCH, DEC_SEQ, N_HEADS, V_DIM))
        outs[5].append(jnp.concatenate([state_pool[l][:, DEC_SEQ:], to_bm(xp_s)], axis=1))
        outs[7].append(to_bm(xr_s)[:, DEC_SEQ - (CONV_WIDTH - 1):])
        outs[9].append(hlast_s.reshape(DEC_BATCH, REC_WIDTH))

    return (xp_rows.reshape(BATCH, SEQ, D_MODEL), to_bm(xs_rows)) + tuple(jnp.stack(o) for o in outs)
```

```python
import functools
import math

import jax
import jax.numpy as jnp
import numpy as np
from jax import lax
from jax.experimental import pallas as pl
from jax.experimental.pallas import tpu as pltpu

D_MODEL = 1024
BATCH = 4
SEQ = 4096
DEPTH = 2
DEC_BATCH = 32
DEC_SEQ = 4
PAST_LEN = 16384
PAGE_SIZE = 128
N_PAGES = PAST_LEN // PAGE_SIZE

POOL_WIDTH = 256
POOL_WINDOWS = (2, 4, 8, 16)
POOL_GROUP = 64
POOL_HIST = 15
REC_WIDTH = 256
CONV_WIDTH = 4
RGLRU_C = 8.0
N_HEADS = 4
QK_DIM = 64
V_DIM = 128
ATTN_WIDTH = N_HEADS * V_DIM
ROPE_DIM = 16
ROPE_THETA = 500000.0
ATTN_SCALE = 1.0 / math.sqrt(QK_DIM)
Q_PRESCALE = ATTN_SCALE * math.log2(math.e)
QK_WIDTH = N_HEADS * 2 * QK_DIM
IN_WIDTH = POOL_WIDTH + 2 * REC_WIDTH + 2 * QK_WIDTH + ATTN_WIDTH
D_FF = 4 * D_MODEL
EPS = 1e-6

F32 = jnp.float32
BF16 = jnp.bfloat16
NEG_BIG = -1e30
V7X_VMEM_LIMIT = 56 * 1024 * 1024

PROJ_TM = 512
MIX_TT = 512
ATTN_TQ = 1024
ATTN_ROW_CHUNK = 256
PAGE_GROUP = 8
N_GROUPS = N_PAGES // PAGE_GROUP
N_SLOTS = 4
N_ROUNDS = N_GROUPS // N_SLOTS
GROUPS_PER_FF_CHUNK = 2
N_FF_CHUNKS = N_GROUPS // GROUPS_PER_FF_CHUNK
FF_CHUNK = D_FF // N_FF_CHUNKS


def _bdot(a, b):
    return jnp.dot(a.astype(BF16), b.astype(BF16), preferred_element_type=F32)


def _params(n_axes):
    return pltpu.CompilerParams(dimension_semantics=("arbitrary",) * n_axes,
                                vmem_limit_bytes=V7X_VMEM_LIMIT)


def _ada_kernel(c_ref, w_ref, b_ref, o_ref):
    c = c_ref[...]
    o_ref[...] = _bdot(c * jax.nn.sigmoid(c), w_ref[...]) + b_ref[...]


def _ada_call(c_all, w_ada, b_ada, *, layer):
    rows = c_all.shape[0]
    tn = 1536
    return pl.pallas_call(
        _ada_kernel,
        grid=(6 * D_MODEL // tn,),
        in_specs=[pl.BlockSpec((rows, D_MODEL), lambda j: (0, 0)),
                  pl.BlockSpec((None, D_MODEL, tn), lambda j: (layer, 0, j)),
                  pl.BlockSpec((None, 1, tn), lambda j: (layer, 0, j))],
        out_specs=pl.BlockSpec((rows, tn), lambda j: (0, j)),
        out_shape=jax.ShapeDtypeStruct((rows, 6 * D_MODEL), F32),
        compiler_params=_params(1),
        name="ada",
    )(c_all, w_ada, b_ada.reshape(DEPTH, 1, 6 * D_MODEL))


def _proj_kernel(x_ref, mod_ref, gmix_ref, w_ref, cos_ref, sin_ref, qg_ref, kg_ref, grp_ref, *refs):
    xp_ref, xr_ref, gate_ref, q_ref, kt_ref, v_ref = refs[-6:]
    x = x_ref[...]
    y = x * lax.rsqrt(jnp.mean(x * x, axis=-1, keepdims=True) + EPS) * gmix_ref[...]
    sh1 = mod_ref[:, 0:D_MODEL]
    sc1 = mod_ref[:, D_MODEL:2 * D_MODEL]
    h = y * (1.0 + sc1) + sh1
    proj = _bdot(h, w_ref[...])
    xp_ref[...] = proj[:, 0:256]
    xr_ref[...] = proj[:, 256:512]
    gate_ref[...] = proj[:, 512:768]
    tm = x.shape[0]
    for hd in range(N_HEADS):
        v_ref[pl.ds(hd, tm, stride=N_HEADS), :] = proj[:, 1792 + hd * V_DIM:1792 + (hd + 1) * V_DIM]

    cos = jnp.concatenate([cos_ref[...]] * N_HEADS, axis=1)
    sin = jnp.concatenate([sin_ref[...]] * N_HEADS, axis=1)
    lane = lax.broadcasted_iota(jnp.int32, cos.shape, 1) % QK_DIM
    grp = grp_ref[...]

    def norm_rope(z, g):
        ssq = _bdot(z * z, grp)
        zn = z * lax.rsqrt(ssq * (1.0 / QK_DIM) + EPS) * g
        half = ROPE_DIM // 2
        swapped = jnp.where(lane < half, pltpu.roll(zn, QK_WIDTH - half, 1), pltpu.roll(zn, half, 1))
        return zn * cos + swapped * sin

    q = norm_rope(proj[:, 768:1280], qg_ref[...])
    q_ref[...] = (q * Q_PRESCALE).astype(q_ref.dtype)
    kt_ref[...] = norm_rope(proj[:, 1280:1792], kg_ref[...]).T


def _proj_call(x, mod3, gmix, w_in_bf, cos_t, sin_t, qg, kg, grp, kv_prev, *, tm, tiles_per_seq, layer, n_layers):
    m = x.shape[0]
    n_seq = m // (tm * tiles_per_seq)
    seq_len = tm * tiles_per_seq
    mod_rows = mod3.shape[1]
    row = lambda w: pl.BlockSpec((tm, w), lambda i: (i, 0))
    full = lambda a: pl.BlockSpec(a.shape, lambda i: (0,) * a.ndim)
    ins = [x, mod3, gmix, w_in_bf, cos_t, sin_t, qg, kg, grp]
    in_specs = [row(D_MODEL),
                pl.BlockSpec((None, mod_rows, 6 * D_MODEL), lambda i: (i // tiles_per_seq, 0, 0)),
                full(gmix), full(w_in_bf),
                pl.BlockSpec((tm, V_DIM), lambda i: (i % tiles_per_seq, 0)),
                pl.BlockSpec((tm, V_DIM), lambda i: (i % tiles_per_seq, 0)),
                full(qg), full(kg), full(grp)]
    aliases = {}
    if kv_prev is not None:
        aliases = {len(ins): 4, len(ins) + 1: 5}
        ins += list(kv_prev)
        in_specs += [pl.BlockSpec(memory_space=pl.ANY)] * 2
    return pl.pallas_call(
        _proj_kernel,
        grid=(m // tm,),
        in_specs=in_specs,
        out_specs=[row(256), row(256), row(256), row(512),
                   pl.BlockSpec((None, QK_WIDTH, tm),
                                lambda i: (layer * n_seq + i // tiles_per_seq, 0, i % tiles_per_seq)),
                   pl.BlockSpec((tm * N_HEADS, V_DIM), lambda i: (layer * (m // tm) + i, 0))],
        out_shape=[jax.ShapeDtypeStruct((m, 256), F32)] * 3
                  + [jax.ShapeDtypeStruct((m, 512), BF16),
                     jax.ShapeDtypeStruct((n_layers * n_seq, QK_WIDTH, seq_len), F32),
                     jax.ShapeDtypeStruct((n_layers * m * N_HEADS, V_DIM), F32)],
        input_output_aliases=aliases,
        compiler_params=_params(1),
        name="proj",
    )(*ins)


def _mix_kernel(*refs, stride, rows, pos0, has_hist):
    if has_hist:
        (xp_ref, xr_ref, gate_ref, hp_ref, hc_ref, h0_ref, poolw_ref, pscale_ref, convw_ref, convb_ref,
         wax_ref, bax_ref, lam_ref, ypool_ref, yrec_ref, hlast_ref, pext, cext, hcar) = refs
    else:
        (xp_ref, xr_ref, gate_ref, poolw_ref, pscale_ref, convw_ref, convb_ref,
         wax_ref, bax_ref, lam_ref, ypool_ref, yrec_ref, hlast_ref, pext, cext, hcar) = refs
    t = pl.program_id(1)
    tt = rows // stride
    hp_rows = 16 * stride
    hc_rows = 8 * stride

    @pl.when(t == 0)
    def _():
        pext[0:hp_rows, :] = jnp.zeros((hp_rows, POOL_WIDTH), F32)
        cext[0:hc_rows, :] = jnp.zeros((hc_rows, REC_WIDTH), F32)
        if has_hist:
            pext[stride:hp_rows, :] = hp_ref[...]
            cext[(8 - (CONV_WIDTH - 1)) * stride:hc_rows, :] = hc_ref[...]
            hcar[...] = h0_ref[...]
        else:
            hcar[...] = jnp.zeros((stride, REC_WIDTH), F32)

    @pl.when(t > 0)
    def _():
        pext[0:hp_rows, :] = pext[rows:rows + hp_rows, :]
        cext[0:hc_rows, :] = cext[rows:rows + hc_rows, :]

    pext[hp_rows:hp_rows + rows, :] = xp_ref[...]
    cext[hc_rows:hc_rows + rows, :] = xr_ref[...]

    tidx = lax.broadcasted_iota(jnp.int32, (rows, 128), 0) // stride
    lane = lax.broadcasted_iota(jnp.int32, (rows, 128), 1)
    pos1 = (pos0 + 1 + t * tt + tidx).astype(F32)

    def back(j, lo):
        return pext[hp_rows - j * stride:hp_rows - j * stride + rows, lo:lo + 128]

    def window_sums(lo, w_small):
        acc = back(0, lo)
        for j in range(1, w_small):
            acc = acc + back(j, lo)
        big = acc
        for j in range(w_small, 2 * w_small):
            big = big + back(j, lo)
        cnt = jnp.where(lane < POOL_GROUP, jnp.minimum(float(w_small), pos1),
                        jnp.minimum(float(2 * w_small), pos1))
        return jnp.where(lane < POOL_GROUP, acc, big) / cnt - back(0, lo)

    d = jnp.concatenate([window_sums(0, 2), window_sums(128, 8)], axis=1)
    ypool_ref[...] = (_bdot(d, poolw_ref[...]) * pscale_ref[...]).astype(ypool_ref.dtype)

    y = convb_ref[...]
    for tap in range(CONV_WIDTH):
        off = hc_rows - (CONV_WIDTH - 1 - tap) * stride
        y = y + cext[off:off + rows, :] * convw_ref[tap:tap + 1, :]
    rg = _bdot(y, wax_ref[...]) + bax_ref[...]
    r = jax.nn.sigmoid(rg[:, 0:REC_WIDTH])
    i = jax.nn.sigmoid(rg[:, REC_WIDTH:2 * REC_WIDTH])
    nl = -lam_ref[...]
    softplus = jnp.maximum(nl, 0.0) + jnp.log1p(jnp.exp(-jnp.abs(nl)))
    a = jnp.exp(-RGLRU_C * r * softplus)
    b = jnp.sqrt(1.0 - a * a) * (i * y)

    tfull = lax.broadcasted_iota(jnp.int32, (rows, REC_WIDTH), 0) // stride
    step = 1
    while step < tt:
        a_prev = pltpu.roll(a, step * stride, 0)
        b_prev = pltpu.roll(b, step * stride, 0)
        valid = tfull >= step
        b = jnp.where(valid, a * b_prev + b, b)
        a = jnp.where(valid, a * a_prev, a)
        step *= 2
    hc = hcar[...]
    carry = jnp.broadcast_to(hc, (rows, REC_WIDTH)) if stride == 1 else jnp.concatenate([hc] * tt, axis=0)
    h = b + a * carry
    h_end = h[rows - stride:rows, :]
    hcar[...] = h_end
    hlast_ref[...] = h_end

    g = gate_ref[...]
    gelu = 0.5 * g * (1.0 + jnp.tanh(math.sqrt(2.0 / math.pi) * (g + 0.044715 * (g * g * g))))
    yrec_ref[...] = (h * gelu).astype(yrec_ref.dtype)


def _mix_call(xp, xr, gate, hist, wts, *, n_seq, n_t, stride, rows, pos0):
    m = xp.shape[0]
    has_hist = hist is not None
    row = pl.BlockSpec((rows, 256), lambda b, t: (b * n_t + t, 0))
    full = lambda a: pl.BlockSpec(a.shape, lambda b, t: (0,) * a.ndim)
    ins = [xp, xr, gate] + (list(hist) if has_hist else []) + list(wts)
    in_specs = [row, row, row] + [full(a) for a in ins[3:]]
    return pl.pallas_call(
        functools.partial(_mix_kernel, stride=stride, rows=rows, pos0=pos0, has_hist=has_hist),
        grid=(n_seq, n_t),
        in_specs=in_specs,
        out_specs=[row, row, pl.BlockSpec((None, stride, REC_WIDTH), lambda b, t: (b, 0, 0))],
        out_shape=[jax.ShapeDtypeStruct((m, 256), BF16), jax.ShapeDtypeStruct((m, 256), BF16),
                   jax.ShapeDtypeStruct((n_seq, stride, REC_WIDTH), F32)],
        scratch_shapes=[pltpu.VMEM((16 * stride + rows, POOL_WIDTH), F32),
                        pltpu.VMEM((8 * stride + rows, REC_WIDTH), F32),
                        pltpu.VMEM((stride, REC_WIDTH), F32)],
        compiler_params=_params(2),
        name="mix",
    )(*ins)


def _diff_lambda(lamp_ref, lam_init):
    lp = lamp_ref[...]
    s1 = jnp.sum(lp[0:1, :] * lp[1:2, :], axis=-1, keepdims=True)
    s2 = jnp.sum(lp[2:3, :] * lp[3:4, :], axis=-1, keepdims=True)
    return jnp.exp(s1) - jnp.exp(s2) + lam_init


def _subln(o, g, lam_init):
    return o * lax.rsqrt(jnp.mean(o * o, axis=-1, keepdims=True) + EPS) * g * (1.0 - lam_init)


def _attn_prompt_kernel(q_ref, k_ref, v_ref, lamp_ref, subln_ref, o_ref, kb, vb, q2, m_s, acc_s, *,
                        tq, chunk, lam_init):
    qi = pl.program_id(2)

    @pl.when(qi == 0)
    def _():
        for j in range(SEQ // tq):
            kb[j] = k_ref[:, j * tq:(j + 1) * tq].astype(BF16)
        head = pl.program_id(1)
        vb[:, 0:V_DIM] = v_ref[pl.ds(head, SEQ, stride=N_HEADS), :].astype(BF16)
        vb[:, V_DIM:2 * V_DIM] = jnp.ones((SEQ, V_DIM), BF16)

    q = q_ref[...]
    lane = lax.broadcasted_iota(jnp.int32, (tq, 2 * QK_DIM), 1)
    zero = jnp.zeros_like(q)
    q2[0:tq, :] = jnp.where(lane < QK_DIM, q, zero)
    q2[tq:2 * tq, :] = jnp.where(lane >= QK_DIM, q, zero)
    m_s[...] = jnp.full((2 * tq, 128), NEG_BIG, F32)
    acc_s[...] = jnp.zeros((2 * tq, 2 * V_DIM), F32)

    def step(kj, diagonal):
        start = pl.multiple_of(kj * tq, tq)
        kt = kb[kj]
        for c0 in range(0, 2 * tq, chunk):
            rows = slice(c0, c0 + chunk)
            nk = (c0 % tq) + chunk if diagonal else tq
            s = jnp.dot(q2[rows, :], kt[:, 0:nk], preferred_element_type=F32)
            if diagonal:
                r = lax.broadcasted_iota(jnp.int32, (chunk, chunk), 0)
                c = lax.broadcasted_iota(jnp.int32, (chunk, chunk), 1)
                tail = jnp.where(c <= r, s[:, nk - chunk:nk], NEG_BIG)
                s = tail if nk == chunk else jnp.concatenate([s[:, 0:nk - chunk], tail], axis=1)
            m_old = m_s[rows, :]
            m_new = jnp.maximum(m_old, jnp.max(s, axis=-1, keepdims=True))
            e = jnp.concatenate([jnp.exp2(s[:, j:j + 128] - m_new) for j in range(0, nk, 128)], axis=1)
            pv = jnp.dot(e.astype(BF16), vb[pl.ds(start, nk), :], preferred_element_type=F32)
            alpha = jnp.exp2(m_old - m_new)
            acc_s[rows, :] = jnp.concatenate([alpha, alpha], axis=1) * acc_s[rows, :] + pv
            m_s[rows, :] = m_new

    def body(kj, carry):
        step(kj, False)
        return carry

    lax.fori_loop(0, qi, body, 0)
    step(qi, True)

    a0 = acc_s[0:tq, :]
    a1 = acc_s[tq:2 * tq, :]
    lam = _diff_lambda(lamp_ref, lam_init)
    o = a0[:, 0:V_DIM] / a0[:, V_DIM:2 * V_DIM] - lam * (a1[:, 0:V_DIM] / a1[:, V_DIM:2 * V_DIM])
    o_ref[...] = _subln(o, subln_ref[...], lam_init).astype(o_ref.dtype)


def _attn_prompt_call(q, kt_all, v_all, lamp, subln, *, layer, lam_init):
    tq = ATTN_TQ
    qspec = pl.BlockSpec((None, tq, V_DIM), lambda b, h, i: (b, i, h))
    full = lambda a: pl.BlockSpec(a.shape, lambda b, h, i: (0,) * a.ndim)
    return pl.pallas_call(
        functools.partial(_attn_prompt_kernel, tq=tq, chunk=ATTN_ROW_CHUNK, lam_init=lam_init),
        grid=(BATCH, N_HEADS, SEQ // tq),
        in_specs=[qspec,
                  pl.BlockSpec((None, 2 * QK_DIM, SEQ), lambda b, h, i: (layer * BATCH + b, h, 0)),
                  pl.BlockSpec((None, SEQ * N_HEADS, V_DIM), lambda b, h, i: (layer * BATCH + b, 0, 0)),
                  full(lamp), full(subln)],
        out_specs=qspec,
        out_shape=jax.ShapeDtypeStruct((BATCH, SEQ, ATTN_WIDTH), BF16),
        scratch_shapes=[pltpu.VMEM((SEQ // tq, 2 * QK_DIM, tq), BF16),
                        pltpu.VMEM((SEQ, 2 * V_DIM), BF16),
                        pltpu.VMEM((2 * tq, 2 * QK_DIM), BF16),
                        pltpu.VMEM((2 * tq, 128), F32),
                        pltpu.VMEM((2 * tq, 2 * V_DIM), F32)],
        compiler_params=_params(3),
        name="attn_prompt",
    )(q, kt_all, v_all, lamp, subln)


N_SROWS = DEC_SEQ * N_HEADS * 2


def _sample_query_rows(q):
    r = lax.broadcasted_iota(jnp.int32, (N_SROWS, QK_WIDTH), 0)
    g = lax.broadcasted_iota(jnp.int32, (N_SROWS, QK_WIDTH), 1) // QK_DIM
    rows = jnp.zeros((N_SROWS, QK_WIDTH), F32)
    for t in range(DEC_SEQ):
        rows = jnp.where(r % DEC_SEQ == t, jnp.broadcast_to(q[t:t + 1, :], (N_SROWS, QK_WIDTH)), rows)
    return jnp.where(g == r // DEC_SEQ, rows, 0.0).astype(BF16)


def _sample_softmax_update(state, s, v_heads):
    m_old, l_old, acc_old = state
    m_new = jnp.maximum(m_old, jnp.max(s, axis=-1, keepdims=True))
    alpha = jnp.exp2(m_old - m_new)
    e = jnp.concatenate([jnp.exp2(s[:, j:j + 128] - m_new) for j in range(0, s.shape[1], 128)], axis=1)
    l_new = alpha * l_old + jnp.sum(e, axis=-1, keepdims=True)
    eb = e.astype(BF16)
    pv = [jnp.dot(eb[8 * h:8 * h + 8, :], v_heads[h], preferred_element_type=F32) for h in range(N_HEADS)]
    return m_new, l_new, alpha * acc_old + jnp.concatenate(pv, axis=0)


def _sample_new_rows_and_finish(state, qbd, kn_ref, vn_ref, knp_ref, vnp_ref, lamp_ref, subln_ref, os_ref, lam_init):
    knp_ref[...] = jnp.zeros((PAGE_SIZE, QK_WIDTH), F32)
    vnp_ref[...] = jnp.zeros((PAGE_SIZE, ATTN_WIDTH), F32)
    knp_ref[0:DEC_SEQ, :] = kn_ref[...]
    vnp_ref[0:DEC_SEQ, :] = vn_ref[...]
    knp = knp_ref[...].astype(BF16)
    vnp = vnp_ref[...].astype(BF16)
    s_new = lax.dot_general(qbd, knp, (((1,), (1,)), ((), ())), preferred_element_type=F32)
    row_t = lax.broadcasted_iota(jnp.int32, (N_SROWS, PAGE_SIZE), 0) % DEC_SEQ
    col = lax.broadcasted_iota(jnp.int32, (N_SROWS, PAGE_SIZE), 1)
    state = _sample_softmax_update(state, jnp.where(col <= row_t, s_new, NEG_BIG),
                                   [vnp[:, h * V_DIM:(h + 1) * V_DIM] for h in range(N_HEADS)])
    _, l, acc = state
    on = acc / l
    lam = _diff_lambda(lamp_ref, lam_init)
    gain = subln_ref[...]
    for h in range(N_HEADS):
        blk = on[8 * h:8 * h + 8, :]
        diff = blk - lam * pltpu.roll(blk, DEC_SEQ, 0)
        os_ref[:, h * V_DIM:(h + 1) * V_DIM] = _subln(diff, gain, lam_init)[0:DEC_SEQ, :]


def _residual_and_mlp_input(x_ref, yp_ref, yr_ref, o_ref, mod_ref, wout_ref, gmlp_ref):
    mix = (jnp.dot(yp_ref[...].astype(BF16), wout_ref[0:256, :], preferred_element_type=F32)
           + jnp.dot(yr_ref[...].astype(BF16), wout_ref[256:512, :], preferred_element_type=F32)
           + jnp.dot(o_ref[...].astype(BF16), wout_ref[512:1024, :], preferred_element_type=F32))
    g1 = mod_ref[:, 2 * D_MODEL:3 * D_MODEL]
    sh2 = mod_ref[:, 3 * D_MODEL:4 * D_MODEL]
    sc2 = mod_ref[:, 4 * D_MODEL:5 * D_MODEL]
    x1 = x_ref[...] + g1 * mix
    y = x1 * lax.rsqrt(jnp.mean(x1 * x1, axis=-1, keepdims=True) + EPS) * gmlp_ref[...]
    return x1, (y * (1.0 + sc2) + sh2).astype(BF16)


def _mlp_chunk(h2, wup_ref, wdown_ref, c):
    up = jnp.dot(h2, wup_ref[c], preferred_element_type=F32)
    act = jnp.square(jnp.maximum(up, 0.0)).astype(BF16)
    return jnp.dot(act, wdown_ref[c], preferred_element_type=F32)


def _out_kernel(x_ref, yp_ref, yr_ref, o_ref, mod_ref, wout_ref, gmlp_ref, wup_ref, wdown_ref, xo_ref):
    x1, h2 = _residual_and_mlp_input(x_ref, yp_ref, yr_ref, o_ref, mod_ref, wout_ref, gmlp_ref)
    mlp = _mlp_chunk(h2, wup_ref, wdown_ref, 0)
    for c in range(1, N_FF_CHUNKS):
        mlp = mlp + _mlp_chunk(h2, wup_ref, wdown_ref, c)
    xo_ref[...] = x1 + mod_ref[:, 5 * D_MODEL:6 * D_MODEL] * mlp


def _out_call(x, yp, yr, o, mod3, wout_bf, gmlp, wup3, wdown3, *, tm, tiles_per_seq):
    m = x.shape[0]
    mod_rows = mod3.shape[1]
    row = lambda w: pl.BlockSpec((tm, w), lambda i: (i, 0))
    full = lambda a: pl.BlockSpec(a.shape, lambda i: (0,) * a.ndim, pipeline_mode=pl.Buffered(1))
    return pl.pallas_call(
        _out_kernel,
        grid=(m // tm,),
        in_specs=[row(D_MODEL), row(256), row(256), row(512),
                  pl.BlockSpec((None, mod_rows, 6 * D_MODEL), lambda i: (i // tiles_per_seq, 0, 0)),
                  full(wout_bf), full(gmlp), full(wup3), full(wdown3)],
        out_specs=row(D_MODEL),
        out_shape=jax.ShapeDtypeStruct((m, D_MODEL), F32),
        compiler_params=_params(1),
        name="out",
    )(x, yp, yr, o, mod3, wout_bf, gmlp, wup3, wdown3)


def _out_attn_kernel(pt_ref, x_ref, yp_ref, yr_ref, o_ref, mod_ref, wout_ref, gmlp_ref, wup_ref, wdown_ref,
                     q_ref, kn_ref, vn_ref, lamp_ref, subln_ref, ck_ref, cv_ref,
                     xo_ref, os_ref,
                     kbuf, vbuf, sem, h2_s, mlp_s, qbd_s, knp_s, vnp_s, *, layer, lam_init):
    i = pl.program_id(0)
    n_steps = pl.num_programs(0)

    def page_copies(seq, group, slot):
        copies = []
        for j in range(PAGE_GROUP):
            page = pt_ref[seq, group * PAGE_GROUP + j]
            copies.append(pltpu.make_async_copy(ck_ref.at[layer, page], kbuf.at[slot, j], sem.at[0, slot]))
            copies.append(pltpu.make_async_copy(cv_ref.at[layer, page], vbuf.at[slot, j], sem.at[1, slot]))
        return copies

    @pl.when(i == 0)
    def _():
        for slot in range(N_SLOTS):
            for c in page_copies(0, slot, slot):
                c.start()

    x1, h2 = _residual_and_mlp_input(x_ref, yp_ref, yr_ref, o_ref, mod_ref, wout_ref, gmlp_ref)
    xo_ref[...] = x1
    h2_s[...] = h2
    mlp_s[...] = jnp.zeros(mlp_s.shape, F32)
    qbd_s[...] = _sample_query_rows(q_ref[...])

    def page_group(slot, state):
        kt = jnp.concatenate([kbuf[slot, j].reshape(QK_WIDTH, PAGE_SIZE).astype(BF16)
                              for j in range(PAGE_GROUP)], axis=1)
        s = jnp.dot(qbd_s[...], kt, preferred_element_type=F32)
        v_heads = [jnp.concatenate([vbuf[slot, j, pl.ds(h, PAGE_SIZE, stride=N_HEADS), :].astype(BF16)
                                    for j in range(PAGE_GROUP)], axis=0) for h in range(N_HEADS)]
        return _sample_softmax_update(state, s, v_heads)

    def round_of_slots(r, state):
        last = r == N_ROUNDS - 1
        for slot in range(N_SLOTS):
            group = N_SLOTS * r + slot
            for c in page_copies(i, group, slot):
                c.wait()
            if slot % GROUPS_PER_FF_CHUNK == 0:
                mlp_s[...] += _mlp_chunk(h2_s[...], wup_ref, wdown_ref, group // GROUPS_PER_FF_CHUNK)
            state = page_group(slot, state)
            nxt_seq = jnp.where(last, i + 1, i)
            nxt_group = jnp.where(last, slot, group + N_SLOTS)

            @pl.when(jnp.logical_or(jnp.logical_not(last), i + 1 < n_steps))
            def _():
                for c in page_copies(nxt_seq, nxt_group, slot):
                    c.start()
        return state

    state0 = (jnp.full((N_SROWS, 128), NEG_BIG, F32), jnp.zeros((N_SROWS, 128), F32),
              jnp.zeros((N_SROWS, V_DIM), F32))
    state = lax.fori_loop(0, N_ROUNDS, round_of_slots, state0)

    _sample_new_rows_and_finish(state, qbd_s[...], kn_ref, vn_ref, knp_s, vnp_s, lamp_ref, subln_ref, os_ref, lam_init)
    xo_ref[...] = xo_ref[...] + mod_ref[:, 5 * D_MODEL:6 * D_MODEL] * mlp_s[...]


def _out_attn_call(page_table, x, yp, yr, o, mod3, wout_bf, gmlp, wup3, wdown3,
                   q_bm, k_bm, v_bm, lamp, subln, cache_kt, cache_vr, *, tm, tiles_per_seq, layer, lam_init):
    m = x.shape[0]
    assert m // tm == DEC_BATCH, "one prompt row tile per sample sequence"
    row = lambda w: pl.BlockSpec((tm, w), lambda i, pt: (i, 0))
    full = lambda a: pl.BlockSpec(a.shape, lambda i, pt: (0,) * a.ndim, pipeline_mode=pl.Buffered(1))
    tok = pl.BlockSpec((None, DEC_SEQ, QK_WIDTH), lambda i, pt: (i, 0, 0))
    hbm = pl.BlockSpec(memory_space=pl.ANY)
    grid_spec = pltpu.PrefetchScalarGridSpec(
        num_scalar_prefetch=1,
        grid=(m // tm,),
        in_specs=[row(D_MODEL), row(256), row(256), row(512),
                  pl.BlockSpec((None, 1, 6 * D_MODEL), lambda i, pt: (i // tiles_per_seq, 0, 0)),
                  full(wout_bf), full(gmlp), full(wup3), full(wdown3),
                  tok, tok, tok, full(lamp), full(subln), hbm, hbm],
        out_specs=[row(D_MODEL), tok],
        scratch_shapes=[pltpu.VMEM((N_SLOTS, PAGE_GROUP, N_HEADS, 2, QK_DIM, PAGE_SIZE), F32),
                        pltpu.VMEM((N_SLOTS, PAGE_GROUP, PAGE_SIZE * N_HEADS, V_DIM), F32),
                        pltpu.SemaphoreType.DMA((2, N_SLOTS)),
                        pltpu.VMEM((tm, D_MODEL), BF16),
                        pltpu.VMEM((tm, D_MODEL), F32),
                        pltpu.VMEM((N_SROWS, QK_WIDTH), BF16),
                        pltpu.VMEM((PAGE_SIZE, QK_WIDTH), F32),
                        pltpu.VMEM((PAGE_SIZE, ATTN_WIDTH), F32)])
    return pl.pallas_call(
        functools.partial(_out_attn_kernel, layer=layer, lam_init=lam_init),
        grid_spec=grid_spec,
        out_shape=[jax.ShapeDtypeStruct((m, D_MODEL), F32),
                   jax.ShapeDtypeStruct((DEC_BATCH, DEC_SEQ, ATTN_WIDTH), F32)],
        compiler_params=_params(1),
        name="out_attn",
    )(page_table, x, yp, yr, o, mod3, wout_bf, gmlp, wup3, wdown3, q_bm, k_bm, v_bm, lamp, subln,
      cache_kt, cache_vr)


def _block_diag(w):
    g, c, d = w.shape
    eye = jnp.eye(g, dtype=w.dtype)
    return (eye[:, None, :, None] * w[:, :, None, :]).reshape(g * c, g * d)


def _rope_tables(pos):
    half = ROPE_DIM // 2
    freqs = ROPE_THETA ** (-np.arange(half, dtype=np.float64) / half)
    ang = np.asarray(pos, np.float64)[:, None] * freqs[None, :]
    cos, sin = np.cos(ang), np.sin(ang)
    n = ang.shape[0]
    c64 = np.concatenate([cos, cos, np.ones((n, QK_DIM - ROPE_DIM))], axis=1)
    s64 = np.concatenate([-sin, sin, np.zeros((n, QK_DIM - ROPE_DIM))], axis=1)
    return (jnp.asarray(np.concatenate([c64, c64], axis=1), F32),
            jnp.asarray(np.concatenate([s64, s64], axis=1), F32))


def kernel(x_prompt, x_sample, cache_k, cache_v, page_table, state_pool, state_conv, state_rglru,
           c_prompt, c_sample, w_ada, b_ada, g_mix, w_in, pool_w, pool_scale, conv_w, conv_b,
           rg_wa, rg_ba, rg_wx, rg_bx, rg_lambda, q_norm, k_norm, lambda_q1, lambda_k1,
           lambda_q2, lambda_k2, subln, w_out, g_mlp, w_up, w_down):
    n_pool = cache_k.shape[1]
    mp = BATCH * SEQ
    ms = DEC_BATCH * DEC_SEQ
    tm_tiles = SEQ // PROJ_TM

    cos_p, sin_p = _rope_tables(np.arange(SEQ))
    cos_s, sin_s = _rope_tables(PAST_LEN + np.repeat(np.arange(DEC_SEQ), DEC_BATCH))
    grp = _block_diag(jnp.ones((QK_WIDTH // QK_DIM, QK_DIM, QK_DIM), BF16))
    cache_kt = cache_k.transpose(0, 1, 3, 4, 5, 2)
    cache_vr = cache_v.reshape(DEPTH, n_pool, PAGE_SIZE * N_HEADS, V_DIM)
    pad_rows = (-(BATCH + DEC_BATCH)) % 16
    c_all = jnp.concatenate([c_prompt, c_sample, jnp.zeros((pad_rows, D_MODEL), F32)], axis=0)

    def to_tm(a):
        return a.transpose(1, 0, 2).reshape(ms, a.shape[-1])

    def to_bm(a):
        return a.reshape(DEC_SEQ, DEC_BATCH, a.shape[-1]).transpose(1, 0, 2)

    xp_rows = x_prompt.reshape(mp, D_MODEL)
    xs_rows = to_tm(x_sample)
    outs = [[] for _ in range(10)]
    kv_all = (jnp.zeros((DEPTH * BATCH, QK_WIDTH, SEQ), F32), jnp.zeros((DEPTH * mp * N_HEADS, V_DIM), F32))
    for l in range(DEPTH):
        lam_init = 0.8 - 0.6 * math.exp(-0.3 * l)
        mod = _ada_call(c_all, w_ada, b_ada, layer=l)
        mod_p = mod[0:BATCH].reshape(BATCH, 1, 6 * D_MODEL)
        mod_s = jnp.tile(mod[BATCH:BATCH + DEC_BATCH], (DEC_SEQ, 1)).reshape(1, ms, 6 * D_MODEL)
        gmix = g_mix[l].reshape(1, D_MODEL)
        gmlp = g_mlp[l].reshape(1, D_MODEL)
        w_in_bf = w_in[l].astype(BF16)
        wout_bf = w_out[l].astype(BF16)
        wup3 = w_up[l].astype(BF16).reshape(D_MODEL, N_FF_CHUNKS, FF_CHUNK).transpose(1, 0, 2)
        wdown3 = w_down[l].astype(BF16).reshape(N_FF_CHUNKS, FF_CHUNK, D_MODEL)
        qg = jnp.tile(q_norm[l], QK_WIDTH // QK_DIM).reshape(1, QK_WIDTH)
        kg = jnp.tile(k_norm[l], QK_WIDTH // QK_DIM).reshape(1, QK_WIDTH)
        mix_wts = (_block_diag(pool_w[l]).astype(BF16), pool_scale[l].reshape(1, -1),
                   conv_w[l], conv_b[l].reshape(1, -1),
                   jnp.concatenate([_block_diag(rg_wa[l]), _block_diag(rg_wx[l])], axis=1).astype(BF16),
                   jnp.concatenate([rg_ba[l], rg_bx[l]]).reshape(1, -1),
                   rg_lambda[l].reshape(1, -1))
        lamp = jnp.stack([lambda_q1[l], lambda_k1[l], lambda_q2[l], lambda_k2[l]])
        sub_g = subln[l].reshape(1, V_DIM)

        xp, xr, gate, q, kt_all, v_all = _proj_call(xp_rows, mod_p, gmix, w_in_bf, cos_p, sin_p, qg, kg, grp, kv_all,
                                                    tm=PROJ_TM, tiles_per_seq=tm_tiles, layer=l, n_layers=DEPTH)
        kv_all = (kt_all, v_all)
        ypool, yrec, hlast = _mix_call(xp, xr, gate, None, mix_wts, n_seq=BATCH, n_t=SEQ // MIX_TT,
                                       stride=1, rows=MIX_TT, pos0=0)
        o = _attn_prompt_call(q.reshape(BATCH, SEQ, QK_WIDTH), kt_all,
                              v_all.reshape(DEPTH * BATCH, SEQ * N_HEADS, V_DIM), lamp, sub_g,
                              layer=l, lam_init=lam_init)
        outs[4].append(xp.reshape(BATCH, SEQ, POOL_WIDTH)[:, SEQ - POOL_HIST:])
        outs[6].append(xr.reshape(BATCH, SEQ, REC_WIDTH)[:, SEQ - (CONV_WIDTH - 1):])
        outs[8].append(hlast.reshape(BATCH, REC_WIDTH))

        xp_s, xr_s, gate_s, q_s, kt_s, v_s = _proj_call(xs_rows, mod_s, gmix, w_in_bf, cos_s, sin_s, qg, kg, grp, None,
                                                         tm=ms, tiles_per_seq=1, layer=0, n_layers=1)
        k_s = kt_s[0].T
        hist = (state_pool[l].transpose(1, 0, 2).reshape(POOL_HIST * DEC_BATCH, POOL_WIDTH),
                state_conv[l].transpose(1, 0, 2).reshape((CONV_WIDTH - 1) * DEC_BATCH, REC_WIDTH),
                state_rglru[l])
        ypool_s, yrec_s, hlast_s = _mix_call(xp_s, xr_s, gate_s, hist, mix_wts, n_seq=1, n_t=1,
                                             stride=DEC_BATCH, rows=ms, pos0=PAST_LEN)
        k_bm, v_bm = to_bm(k_s), to_bm(v_s.reshape(ms, ATTN_WIDTH))

        xp_rows, o_s = _out_attn_call(page_table, xp_rows, ypool, yrec, o.reshape(mp, ATTN_WIDTH), mod_p, wout_bf,
                                      gmlp, wup3, wdown3, to_bm(q_s.astype(F32)), k_bm, v_bm, lamp, sub_g,
                                      cache_kt, cache_vr, tm=PROJ_TM, tiles_per_seq=tm_tiles, layer=l,
                                      lam_init=lam_init)
        xs_rows = _out_call(xs_rows, ypool_s, yrec_s, to_tm(o_s), mod_s, wout_bf, gmlp,
                            wup3, wdown3, tm=ms, tiles_per_seq=1)
        outs[2].append(k_bm.reshape(DEC_BATCH, DEC_SEQ, N_HEADS, 2, QK_DIM))
        outs[3].append(v_bm.reshape(DEC_BATCH, DEC_SEQ, N_HEADS, V_DIM))
        outs[5].append(jnp.concatenate([state_pool[l][:, DEC_SEQ:], to_bm(xp_s)], axis=1))
        outs[7].append(to_bm(xr_s)[:, DEC_SEQ - (CONV_WIDTH - 1):])
        outs[9].append(hlast_s.reshape(DEC_BATCH, REC_WIDTH))

    kt_all, v_all = kv_all
    new_k_prompt = kt_all.reshape(DEPTH, BATCH, N_HEADS, 2, QK_DIM, SEQ).transpose(0, 1, 5, 2, 3, 4)
    new_v_prompt = v_all.reshape(DEPTH, BATCH, SEQ, N_HEADS, V_DIM)
    stacked = [jnp.stack(o) for o in outs[2:]]
    return (xp_rows.reshape(BATCH, SEQ, D_MODEL), to_bm(xs_rows), new_k_prompt, new_v_prompt) + tuple(stacked)
```

```python
import functools
import math

import jax
import jax.numpy as jnp
import numpy as np
from jax import lax
from jax.experimental import pallas as pl
from jax.experimental.pallas import tpu as pltpu

D_MODEL = 1024
BATCH = 4
SEQ = 4096
DEPTH = 2
DEC_BATCH = 32
DEC_SEQ = 4
PAST_LEN = 16384
PAGE_SIZE = 128
N_PAGES = PAST_LEN // PAGE_SIZE

POOL_WIDTH = 256
POOL_WINDOWS = (2, 4, 8, 16)
POOL_GROUP = 64
POOL_HIST = 15
REC_WIDTH = 256
CONV_WIDTH = 4
RGLRU_C = 8.0
N_HEADS = 4
QK_DIM = 64
V_DIM = 128
ATTN_WIDTH = N_HEADS * V_DIM
ROPE_DIM = 16
ROPE_THETA = 500000.0
ATTN_SCALE = 1.0 / math.sqrt(QK_DIM)
Q_PRESCALE = ATTN_SCALE * math.log2(math.e)
QK_WIDTH = N_HEADS * 2 * QK_DIM
IN_WIDTH = POOL_WIDTH + 2 * REC_WIDTH + 2 * QK_WIDTH + ATTN_WIDTH
D_FF = 4 * D_MODEL
EPS = 1e-6

F32 = jnp.float32
BF16 = jnp.bfloat16
NEG_BIG = -1e30
V7X_VMEM_LIMIT = 56 * 1024 * 1024

PROJ_TM = 512
MIX_TT = 512
ATTN_TQ = 1024
ATTN_ROW_CHUNK = 256
ATTN_DIAG_CHUNK = 512
PAGE_GROUP = 8
N_GROUPS = N_PAGES // PAGE_GROUP
N_SLOTS = 4
N_ROUNDS = N_GROUPS // N_SLOTS
GROUPS_PER_FF_CHUNK = 2
N_FF_CHUNKS = N_GROUPS // GROUPS_PER_FF_CHUNK
FF_CHUNK = D_FF // N_FF_CHUNKS


def _bdot(a, b):
    return jnp.dot(a.astype(BF16), b.astype(BF16), preferred_element_type=F32)


def _params(n_axes):
    return pltpu.CompilerParams(dimension_semantics=("arbitrary",) * n_axes,
                                vmem_limit_bytes=V7X_VMEM_LIMIT)


def _ada_kernel(c_ref, w_ref, b_ref, o_ref):
    c = c_ref[...]
    o_ref[...] = _bdot(c * jax.nn.sigmoid(c), w_ref[...]) + b_ref[...]


def _ada_call(c_all, w_ada, b_ada, *, layer):
    rows = c_all.shape[0]
    tn = 1536
    return pl.pallas_call(
        _ada_kernel,
        grid=(6 * D_MODEL // tn,),
        in_specs=[pl.BlockSpec((rows, D_MODEL), lambda j: (0, 0)),
                  pl.BlockSpec((None, D_MODEL, tn), lambda j: (layer, 0, j)),
                  pl.BlockSpec((None, 1, tn), lambda j: (layer, 0, j))],
        out_specs=pl.BlockSpec((rows, tn), lambda j: (0, j)),
        out_shape=jax.ShapeDtypeStruct((rows, 6 * D_MODEL), F32),
        compiler_params=_params(1),
        name="ada",
    )(c_all, w_ada, b_ada.reshape(DEPTH, 1, 6 * D_MODEL))


def _proj_body(x_ref, mod_ref, gmix_ref, w_ref, cos_ref, sin_ref, qg_ref, kg_ref, grp_ref,
               xp_ref, xr_ref, gate_ref, q_ref, kt_ref, v_ref):
    x = x_ref[...]
    y = x * lax.rsqrt(jnp.mean(x * x, axis=-1, keepdims=True) + EPS) * gmix_ref[...]
    sh1 = mod_ref[:, 0:D_MODEL]
    sc1 = mod_ref[:, D_MODEL:2 * D_MODEL]
    h = y * (1.0 + sc1) + sh1
    proj = _bdot(h, w_ref[...])
    xp_ref[...] = proj[:, 0:256]
    xr_ref[...] = proj[:, 256:512]
    gate_ref[...] = proj[:, 512:768]
    tm = x.shape[0]
    for slot in range(v_ref.shape[0]):
        for hd in range(N_HEADS):
            v_ref[slot, pl.ds(hd, tm, stride=N_HEADS), :] = proj[:, 1792 + hd * V_DIM:1792 + (hd + 1) * V_DIM]

    cos = jnp.concatenate([cos_ref[...]] * N_HEADS, axis=1)
    sin = jnp.concatenate([sin_ref[...]] * N_HEADS, axis=1)
    lane = lax.broadcasted_iota(jnp.int32, cos.shape, 1) % QK_DIM
    grp = grp_ref[...]

    def norm_rope(z, g):
        ssq = _bdot(z * z, grp)
        zn = z * lax.rsqrt(ssq * (1.0 / QK_DIM) + EPS) * g
        half = ROPE_DIM // 2
        swapped = jnp.where(lane < half, pltpu.roll(zn, QK_WIDTH - half, 1), pltpu.roll(zn, half, 1))
        return zn * cos + swapped * sin

    q = norm_rope(proj[:, 768:1280], qg_ref[...])
    q_ref[...] = (q * Q_PRESCALE).astype(q_ref.dtype)
    kt = norm_rope(proj[:, 1280:1792], kg_ref[...]).T
    for slot in range(kt_ref.shape[0]):
        kt_ref[slot] = kt


def _proj_kernel(*refs):
    _proj_body(*refs)


def _proj_io(x, mod3, gmix, w_in_bf, cos_t, sin_t, qg, kg, grp, *, tm, tiles_per_seq):
    mod_rows = mod3.shape[1]
    full = lambda a: pl.BlockSpec(a.shape, lambda i: (0,) * a.ndim)
    ins = [x, mod3, gmix, w_in_bf, cos_t, sin_t, qg, kg, grp]
    in_specs = [pl.BlockSpec((tm, D_MODEL), lambda i: (i, 0)),
                pl.BlockSpec((None, mod_rows, 6 * D_MODEL), lambda i: (i // tiles_per_seq, 0, 0)),
                full(gmix), full(w_in_bf),
                pl.BlockSpec((tm, V_DIM), lambda i: (i % tiles_per_seq, 0)),
                pl.BlockSpec((tm, V_DIM), lambda i: (i % tiles_per_seq, 0)),
                full(qg), full(kg), full(grp)]
    return ins, in_specs


def _kv_out(m, tm, tiles_per_seq, layer, n_layers):
    n_seq = m // (tm * tiles_per_seq)
    seq_len = tm * tiles_per_seq
    slots, first = (n_layers, 0) if layer == 0 else (1, layer)
    specs = [pl.BlockSpec((slots, None, QK_WIDTH, tm), lambda i: (first, i // tiles_per_seq, 0, i % tiles_per_seq)),
             pl.BlockSpec((slots, tm * N_HEADS, V_DIM), lambda i: (first, i, 0))]
    shapes = [jax.ShapeDtypeStruct((n_layers, n_seq, QK_WIDTH, seq_len), F32),
              jax.ShapeDtypeStruct((n_layers, m * N_HEADS, V_DIM), F32)]
    return specs, shapes


def _proj_call(x, mod3, gmix, w_in_bf, cos_t, sin_t, qg, kg, grp, *, tm, tiles_per_seq):
    m = x.shape[0]
    ins, in_specs = _proj_io(x, mod3, gmix, w_in_bf, cos_t, sin_t, qg, kg, grp, tm=tm, tiles_per_seq=tiles_per_seq)
    kv_specs, kv_shapes = _kv_out(m, tm, tiles_per_seq, 0, 1)
    row = lambda w: pl.BlockSpec((tm, w), lambda i: (i, 0))
    return pl.pallas_call(
        _proj_kernel,
        grid=(m // tm,),
        in_specs=in_specs,
        out_specs=[row(256), row(256), row(256), row(512)] + kv_specs,
        out_shape=[jax.ShapeDtypeStruct((m, 256), F32)] * 3 + [jax.ShapeDtypeStruct((m, 512), BF16)] + kv_shapes,
        compiler_params=_params(1),
        name="proj",
    )(*ins)


def _mix_kernel(*refs, stride, rows, pos0, has_hist):
    hist = refs[3:6] if has_hist else None
    rest = refs[6:] if has_hist else refs[3:]
    t = pl.program_id(1)
    _mix_prepare(t, hist, *rest[-3:], stride=stride, rows=rows)
    _mix_body(t, refs[0], refs[1], refs[2], *rest, stride=stride, rows=rows, pos0=pos0)


def _mix_prepare(t, hist, pext, cext, hcar, *, stride, rows):
    has_hist = hist is not None
    if has_hist:
        hp_ref, hc_ref, h0_ref = hist
    hp_rows = 16 * stride
    hc_rows = 8 * stride

    @pl.when(t == 0)
    def _():
        pext[0:hp_rows, :] = jnp.zeros((hp_rows, POOL_WIDTH), F32)
        cext[0:hc_rows, :] = jnp.zeros((hc_rows, REC_WIDTH), F32)
        if has_hist:
            pext[stride:hp_rows, :] = hp_ref[...]
            cext[(8 - (CONV_WIDTH - 1)) * stride:hc_rows, :] = hc_ref[...]
            hcar[...] = h0_ref[...]
        else:
            hcar[...] = jnp.zeros((stride, REC_WIDTH), F32)

    @pl.when(t > 0)
    def _():
        pext[0:hp_rows, :] = pext[rows:rows + hp_rows, :]
        cext[0:hc_rows, :] = cext[rows:rows + hc_rows, :]


def _mix_body(t, xp_ref, xr_ref, gate_ref, poolw_ref, pscale_ref, convw_ref, convb_ref,
              wax_ref, bax_ref, lam_ref, ypool_ref, yrec_ref, hlast_ref, pext, cext, hcar, *, stride, rows, pos0,
              scan_scratch=None):
    tt = rows // stride
    hp_rows = 16 * stride
    hc_rows = 8 * stride
    pext[hp_rows:hp_rows + rows, :] = xp_ref[...]
    cext[hc_rows:hc_rows + rows, :] = xr_ref[...]

    tidx = lax.broadcasted_iota(jnp.int32, (rows, 128), 0) // stride
    lane = lax.broadcasted_iota(jnp.int32, (rows, 128), 1)
    pos1 = (pos0 + 1 + t * tt + tidx).astype(F32)

    def back(j, lo):
        return pext[hp_rows - j * stride:hp_rows - j * stride + rows, lo:lo + 128]

    def window_sums(lo, w_small):
        acc = back(0, lo)
        for j in range(1, w_small):
            acc = acc + back(j, lo)
        big = acc
        for j in range(w_small, 2 * w_small):
            big = big + back(j, lo)
        cnt = jnp.where(lane < POOL_GROUP, jnp.minimum(float(w_small), pos1),
                        jnp.minimum(float(2 * w_small), pos1))
        return jnp.where(lane < POOL_GROUP, acc, big) / cnt - back(0, lo)

    d = jnp.concatenate([window_sums(0, 2), window_sums(128, 8)], axis=1)
    ypool_ref[...] = (_bdot(d, poolw_ref[...]) * pscale_ref[...]).astype(ypool_ref.dtype)

    y = convb_ref[...]
    for tap in range(CONV_WIDTH):
        off = hc_rows - (CONV_WIDTH - 1 - tap) * stride
        y = y + cext[off:off + rows, :] * convw_ref[tap:tap + 1, :]
    rg = _bdot(y, wax_ref[...]) + bax_ref[...]
    r = jax.nn.sigmoid(rg[:, 0:REC_WIDTH])
    i = jax.nn.sigmoid(rg[:, REC_WIDTH:2 * REC_WIDTH])
    nl = -lam_ref[...]
    softplus = jnp.maximum(nl, 0.0) + jnp.log1p(jnp.exp(-jnp.abs(nl)))
    a = jnp.exp(-RGLRU_C * r * softplus)
    b = jnp.sqrt(1.0 - a * a) * (i * y)

    def doubling_scan(a, b, idx, n, shift):
        step = 1
        while step < n:
            a_prev = pltpu.roll(a, step * shift, 0)
            b_prev = pltpu.roll(b, step * shift, 0)
            valid = idx >= step
            b = jnp.where(valid, a * b_prev + b, b)
            a = jnp.where(valid, a * a_prev, a)
            step *= 2
        return a, b

    hc = hcar[...]
    row = lax.broadcasted_iota(jnp.int32, (rows, REC_WIDTH), 0)
    if scan_scratch is None:
        a, b = doubling_scan(a, b, row // stride, tt, stride)
        carry = jnp.broadcast_to(hc, (rows, REC_WIDTH)) if stride == 1 else jnp.concatenate([hc] * tt, axis=0)
    else:
        a_s, b_s, hin_s = scan_scratch
        n_groups = rows // 8
        a, b = doubling_scan(a, b, row % 8, 8, 1)
        halves = range(REC_WIDTH // 128)

        def put(ref, val):
            for k in halves:
                ref[k] = val[:, 128 * k:128 * (k + 1)]

        put(a_s, a)
        put(b_s, b)
        ga = jnp.concatenate([a_s[k, pl.ds(7, n_groups, stride=8), :] for k in halves], axis=1)
        gb = jnp.concatenate([b_s[k, pl.ds(7, n_groups, stride=8), :] for k in halves], axis=1)
        gidx = lax.broadcasted_iota(jnp.int32, (n_groups, REC_WIDTH), 0)
        ga, gb = doubling_scan(ga, gb, gidx, n_groups, 1)
        after = gb + ga * hc
        put(hin_s, jnp.where(gidx == 0, jnp.broadcast_to(hc, after.shape), pltpu.roll(after, 1, 0)))
        carry = jnp.concatenate(
            [jnp.concatenate([hin_s[k, pl.ds(g, 8, stride=0), :] for g in range(n_groups)], axis=0) for k in halves],
            axis=1)
    h = b + a * carry
    h_end = h[rows - stride:rows, :]
    hcar[...] = h_end
    hlast_ref[...] = h_end

    g = gate_ref[...]
    gelu = 0.5 * g * (1.0 + jnp.tanh(math.sqrt(2.0 / math.pi) * (g + 0.044715 * (g * g * g))))
    yrec_ref[...] = (h * gelu).astype(yrec_ref.dtype)


def _mix_call(xp, xr, gate, hist, wts, *, n_seq, n_t, stride, rows, pos0):
    m = xp.shape[0]
    has_hist = hist is not None
    row = pl.BlockSpec((rows, 256), lambda b, t: (b * n_t + t, 0))
    full = lambda a: pl.BlockSpec(a.shape, lambda b, t: (0,) * a.ndim)
    ins = [xp, xr, gate] + (list(hist) if has_hist else []) + list(wts)
    in_specs = [row, row, row] + [full(a) for a in ins[3:]]
    return pl.pallas_call(
        functools.partial(_mix_kernel, stride=stride, rows=rows, pos0=pos0, has_hist=has_hist),
        grid=(n_seq, n_t),
        in_specs=in_specs,
        out_specs=[row, row, pl.BlockSpec((None, stride, REC_WIDTH), lambda b, t: (b, 0, 0))],
        out_shape=[jax.ShapeDtypeStruct((m, 256), BF16), jax.ShapeDtypeStruct((m, 256), BF16),
                   jax.ShapeDtypeStruct((n_seq, stride, REC_WIDTH), F32)],
        scratch_shapes=[pltpu.VMEM((16 * stride + rows, POOL_WIDTH), F32),
                        pltpu.VMEM((8 * stride + rows, REC_WIDTH), F32),
                        pltpu.VMEM((stride, REC_WIDTH), F32)],
        compiler_params=_params(2),
        name="mix",
    )(*ins)


N_MIX_WTS = 7


def _proj_mix_kernel(*refs, rows, tiles_per_seq, aliased):
    proj_in = refs[0:9]
    mix_wts = refs[9:9 + N_MIX_WTS]
    outs = refs[9 + N_MIX_WTS + (2 if aliased else 0):]
    q_ref, kt_ref, v_ref, ypool_ref, yrec_ref, hlast_ref, pstate_ref, cstate_ref = outs[0:8]
    xp_s, xr_s, gate_s, pext, cext, hcar, scan_a, scan_b, scan_h = outs[8:]
    t = pl.program_id(0) % tiles_per_seq
    _mix_prepare(t, None, pext, cext, hcar, stride=1, rows=rows)
    _proj_body(*proj_in, xp_s, xr_s, gate_s, q_ref, kt_ref, v_ref)
    _mix_body(t, xp_s, xr_s, gate_s, *mix_wts, ypool_ref, yrec_ref, hlast_ref, pext, cext, hcar,
              stride=1, rows=rows, pos0=0, scan_scratch=(scan_a, scan_b, scan_h))
    pstate_ref[...] = pext[rows:rows + 16, :]
    cstate_ref[...] = cext[rows:rows + 8, :]


def _proj_mix_call(x, mod3, gmix, w_in_bf, cos_t, sin_t, qg, kg, grp, mix_wts, kv_prev, *,
                   tm, tiles_per_seq, layer, n_layers):
    m = x.shape[0]
    n_seq = m // (tm * tiles_per_seq)
    ins, in_specs = _proj_io(x, mod3, gmix, w_in_bf, cos_t, sin_t, qg, kg, grp, tm=tm, tiles_per_seq=tiles_per_seq)
    full = lambda a: pl.BlockSpec(a.shape, lambda i: (0,) * a.ndim)
    ins += list(mix_wts)
    in_specs += [full(a) for a in mix_wts]
    aliases = {}
    if kv_prev is not None:
        aliases = {len(ins): 1, len(ins) + 1: 2}
        ins += list(kv_prev)
        in_specs += [pl.BlockSpec(memory_space=pl.ANY)] * 2
    kv_specs, kv_shapes = _kv_out(m, tm, tiles_per_seq, layer, n_layers)
    row = lambda w: pl.BlockSpec((tm, w), lambda i: (i, 0))
    per_seq = lambda r: pl.BlockSpec((None, r, 256), lambda i: (i // tiles_per_seq, 0, 0))
    return pl.pallas_call(
        functools.partial(_proj_mix_kernel, rows=tm, tiles_per_seq=tiles_per_seq, aliased=kv_prev is not None),
        grid=(m // tm,),
        in_specs=in_specs,
        out_specs=[row(512)] + kv_specs + [row(256), row(256), per_seq(1), per_seq(16), per_seq(8)],
        out_shape=[jax.ShapeDtypeStruct((m, 512), BF16)] + kv_shapes
                  + [jax.ShapeDtypeStruct((m, 256), BF16), jax.ShapeDtypeStruct((m, 256), BF16),
                     jax.ShapeDtypeStruct((n_seq, 1, REC_WIDTH), F32),
                     jax.ShapeDtypeStruct((n_seq, 16, POOL_WIDTH), F32),
                     jax.ShapeDtypeStruct((n_seq, 8, REC_WIDTH), F32)],
        scratch_shapes=[pltpu.VMEM((tm, POOL_WIDTH), F32), pltpu.VMEM((tm, REC_WIDTH), F32),
                        pltpu.VMEM((tm, REC_WIDTH), F32),
                        pltpu.VMEM((16 + tm, POOL_WIDTH), F32), pltpu.VMEM((8 + tm, REC_WIDTH), F32),
                        pltpu.VMEM((1, REC_WIDTH), F32),
                        pltpu.VMEM((REC_WIDTH // 128, tm, 128), F32), pltpu.VMEM((REC_WIDTH // 128, tm, 128), F32),
                        pltpu.VMEM((REC_WIDTH // 128, tm // 8, 128), F32)],
        input_output_aliases=aliases,
        compiler_params=_params(1),
        name="proj_mix",
    )(*ins)


def _diff_lambda(lamp_ref, lam_init):
    lp = lamp_ref[...]
    s1 = jnp.sum(lp[0:1, :] * lp[1:2, :], axis=-1, keepdims=True)
    s2 = jnp.sum(lp[2:3, :] * lp[3:4, :], axis=-1, keepdims=True)
    return jnp.exp(s1) - jnp.exp(s2) + lam_init


def _subln(o, g, lam_init):
    return o * lax.rsqrt(jnp.mean(o * o, axis=-1, keepdims=True) + EPS) * g * (1.0 - lam_init)


def _attn_prompt_kernel(q_ref, k_ref, v_ref, lamp_ref, subln_ref, o_ref, kb, vb, q2, m_s, acc_s, *,
                        tq, full_chunk, diag_chunk, lam_init):
    qi = pl.program_id(2)

    @pl.when(qi == 0)
    def _():
        for j in range(SEQ // tq):
            kb[j] = k_ref[:, j * tq:(j + 1) * tq].astype(BF16)
        head = pl.program_id(1)
        vb[:, 0:V_DIM] = v_ref[pl.ds(head, SEQ, stride=N_HEADS), :].astype(BF16)
        vb[:, V_DIM:2 * V_DIM] = jnp.ones((SEQ, V_DIM), BF16)

    q = q_ref[...]
    lane = lax.broadcasted_iota(jnp.int32, (tq, 2 * QK_DIM), 1)
    zero = jnp.zeros_like(q)
    q2[0:tq, :] = jnp.where(lane < QK_DIM, q, zero)
    q2[tq:2 * tq, :] = jnp.where(lane >= QK_DIM, q, zero)
    m_s[...] = jnp.full((2 * tq, 128), NEG_BIG, F32)
    acc_s[...] = jnp.zeros((2 * tq, 2 * V_DIM), F32)

    def step(kj, diagonal):
        start = pl.multiple_of(kj * tq, tq)
        kt = kb[kj]
        chunk = diag_chunk if diagonal else full_chunk
        for c0 in range(0, 2 * tq, chunk):
            rows = slice(c0, c0 + chunk)
            nk = (c0 % tq) + chunk if diagonal else tq
            s = jnp.dot(q2[rows, :], kt[:, 0:nk], preferred_element_type=F32)
            if diagonal:
                r = lax.broadcasted_iota(jnp.int32, (chunk, chunk), 0)
                c = lax.broadcasted_iota(jnp.int32, (chunk, chunk), 1)
                tail = jnp.where(c <= r, s[:, nk - chunk:nk], NEG_BIG)
                s = tail if nk == chunk else jnp.concatenate([s[:, 0:nk - chunk], tail], axis=1)
            m_old = m_s[rows, :]
            m_new = jnp.maximum(m_old, jnp.max(s, axis=-1, keepdims=True))
            e = jnp.concatenate([jnp.exp2(s[:, j:j + 128] - m_new) for j in range(0, nk, 128)], axis=1)
            pv = jnp.dot(e.astype(BF16), vb[pl.ds(start, nk), :], preferred_element_type=F32)
            alpha = jnp.exp2(m_old - m_new)
            acc_s[rows, :] = jnp.concatenate([alpha, alpha], axis=1) * acc_s[rows, :] + pv
            m_s[rows, :] = m_new

    def body(kj, carry):
        step(kj, False)
        return carry

    lax.fori_loop(0, qi, body, 0)
    step(qi, True)

    a0 = acc_s[0:tq, :]
    a1 = acc_s[tq:2 * tq, :]
    lam = _diff_lambda(lamp_ref, lam_init)
    o = a0[:, 0:V_DIM] / a0[:, V_DIM:2 * V_DIM] - lam * (a1[:, 0:V_DIM] / a1[:, V_DIM:2 * V_DIM])
    o_ref[...] = _subln(o, subln_ref[...], lam_init).astype(o_ref.dtype)


def _attn_prompt_call(q, kt_all, v_all, lamp, subln, *, layer, lam_init):
    tq = ATTN_TQ
    qspec = pl.BlockSpec((None, tq, V_DIM), lambda b, h, i: (b, i, h))
    full = lambda a: pl.BlockSpec(a.shape, lambda b, h, i: (0,) * a.ndim)
    return pl.pallas_call(
        functools.partial(_attn_prompt_kernel, tq=tq, full_chunk=ATTN_ROW_CHUNK, diag_chunk=ATTN_DIAG_CHUNK,
                          lam_init=lam_init),
        grid=(BATCH, N_HEADS, SEQ // tq),
        in_specs=[qspec,
                  pl.BlockSpec((None, None, 2 * QK_DIM, SEQ), lambda b, h, i: (layer, b, h, 0)),
                  pl.BlockSpec((None, None, SEQ * N_HEADS, V_DIM), lambda b, h, i: (layer, b, 0, 0)),
                  full(lamp), full(subln)],
        out_specs=qspec,
        out_shape=jax.ShapeDtypeStruct((BATCH, SEQ, ATTN_WIDTH), BF16),
        scratch_shapes=[pltpu.VMEM((SEQ // tq, 2 * QK_DIM, tq), BF16),
                        pltpu.VMEM((SEQ, 2 * V_DIM), BF16),
                        pltpu.VMEM((2 * tq, 2 * QK_DIM), BF16),
                        pltpu.VMEM((2 * tq, 128), F32),
                        pltpu.VMEM((2 * tq, 2 * V_DIM), F32)],
        compiler_params=_params(3),
        name="attn_prompt",
    )(q, kt_all, v_all, lamp, subln)


N_SROWS = DEC_SEQ * N_HEADS * 2


def _sample_query_rows(q):
    r = lax.broadcasted_iota(jnp.int32, (N_SROWS, QK_WIDTH), 0)
    g = lax.broadcasted_iota(jnp.int32, (N_SROWS, QK_WIDTH), 1) // QK_DIM
    rows = jnp.zeros((N_SROWS, QK_WIDTH), F32)
    for t in range(DEC_SEQ):
        rows = jnp.where(r % DEC_SEQ == t, jnp.broadcast_to(q[t:t + 1, :], (N_SROWS, QK_WIDTH)), rows)
    return jnp.where(g == r // DEC_SEQ, rows, 0.0).astype(BF16)


def _sample_softmax_update(state, s, v_heads):
    m_old, l_old, acc_old = state
    m_new = jnp.maximum(m_old, jnp.max(s, axis=-1, keepdims=True))
    alpha = jnp.exp2(m_old - m_new)
    e = jnp.concatenate([jnp.exp2(s[:, j:j + 128] - m_new) for j in range(0, s.shape[1], 128)], axis=1)
    l_new = alpha * l_old + jnp.sum(e, axis=-1, keepdims=True)
    eb = e.astype(BF16)
    pv = [jnp.dot(eb[8 * h:8 * h + 8, :], v_heads[h], preferred_element_type=F32) for h in range(N_HEADS)]
    return m_new, l_new, alpha * acc_old + jnp.concatenate(pv, axis=0)


def _sample_new_rows_and_finish(state, qbd, kn_ref, vn_ref, knp_ref, vnp_ref, lamp_ref, subln_ref, os_ref, lam_init):
    knp_ref[...] = jnp.zeros((PAGE_SIZE, QK_WIDTH), F32)
    vnp_ref[...] = jnp.zeros((PAGE_SIZE, ATTN_WIDTH), F32)
    knp_ref[0:DEC_SEQ, :] = kn_ref[...]
    vnp_ref[0:DEC_SEQ, :] = vn_ref[...]
    knp = knp_ref[...].astype(BF16)
    vnp = vnp_ref[...].astype(BF16)
    s_new = lax.dot_general(qbd, knp, (((1,), (1,)), ((), ())), preferred_element_type=F32)
    row_t = lax.broadcasted_iota(jnp.int32, (N_SROWS, PAGE_SIZE), 0) % DEC_SEQ
    col = lax.broadcasted_iota(jnp.int32, (N_SROWS, PAGE_SIZE), 1)
    state = _sample_softmax_update(state, jnp.where(col <= row_t, s_new, NEG_BIG),
                                   [vnp[:, h * V_DIM:(h + 1) * V_DIM] for h in range(N_HEADS)])
    _, l, acc = state
    on = acc / l
    lam = _diff_lambda(lamp_ref, lam_init)
    gain = subln_ref[...]
    for h in range(N_HEADS):
        blk = on[8 * h:8 * h + 8, :]
        diff = blk - lam * pltpu.roll(blk, DEC_SEQ, 0)
        os_ref[:, h * V_DIM:(h + 1) * V_DIM] = _subln(diff, gain, lam_init)[0:DEC_SEQ, :]


def _residual_and_mlp_input(x_ref, yp_ref, yr_ref, o_ref, mod_ref, wout_ref, gmlp_ref):
    mix = (jnp.dot(yp_ref[...].astype(BF16), wout_ref[0:256, :], preferred_element_type=F32)
           + jnp.dot(yr_ref[...].astype(BF16), wout_ref[256:512, :], preferred_element_type=F32)
           + jnp.dot(o_ref[...].astype(BF16), wout_ref[512:1024, :], preferred_element_type=F32))
    g1 = mod_ref[:, 2 * D_MODEL:3 * D_MODEL]
    sh2 = mod_ref[:, 3 * D_MODEL:4 * D_MODEL]
    sc2 = mod_ref[:, 4 * D_MODEL:5 * D_MODEL]
    x1 = x_ref[...] + g1 * mix
    y = x1 * lax.rsqrt(jnp.mean(x1 * x1, axis=-1, keepdims=True) + EPS) * gmlp_ref[...]
    return x1, (y * (1.0 + sc2) + sh2).astype(BF16)


def _mlp_chunk(h2, wup_ref, wdown_ref, c):
    up = jnp.dot(h2, wup_ref[c], preferred_element_type=F32)
    act = jnp.square(jnp.maximum(up, 0.0)).astype(BF16)
    return jnp.dot(act, wdown_ref[c], preferred_element_type=F32)


def _out_kernel(x_ref, yp_ref, yr_ref, o_ref, mod_ref, wout_ref, gmlp_ref, wup_ref, wdown_ref, xo_ref):
    x1, h2 = _residual_and_mlp_input(x_ref, yp_ref, yr_ref, o_ref, mod_ref, wout_ref, gmlp_ref)
    mlp = _mlp_chunk(h2, wup_ref, wdown_ref, 0)
    for c in range(1, N_FF_CHUNKS):
        mlp = mlp + _mlp_chunk(h2, wup_ref, wdown_ref, c)
    xo_ref[...] = x1 + mod_ref[:, 5 * D_MODEL:6 * D_MODEL] * mlp


def _out_call(x, yp, yr, o, mod3, wout_bf, gmlp, wup3, wdown3, *, tm, tiles_per_seq):
    m = x.shape[0]
    mod_rows = mod3.shape[1]
    row = lambda w: pl.BlockSpec((tm, w), lambda i: (i, 0))
    full = lambda a: pl.BlockSpec(a.shape, lambda i: (0,) * a.ndim, pipeline_mode=pl.Buffered(1))
    return pl.pallas_call(
        _out_kernel,
        grid=(m // tm,),
        in_specs=[row(D_MODEL), row(256), row(256), row(512),
                  pl.BlockSpec((None, mod_rows, 6 * D_MODEL), lambda i: (i // tiles_per_seq, 0, 0)),
                  full(wout_bf), full(gmlp), full(wup3), full(wdown3)],
        out_specs=row(D_MODEL),
        out_shape=jax.ShapeDtypeStruct((m, D_MODEL), F32),
        compiler_params=_params(1),
        name="out",
    )(x, yp, yr, o, mod3, wout_bf, gmlp, wup3, wdown3)


def _out_attn_kernel(pt_ref, x_ref, yp_ref, yr_ref, o_ref, mod_ref, wout_ref, gmlp_ref, wup_ref, wdown_ref,
                     q_ref, kn_ref, vn_ref, lamp_ref, subln_ref, ck_ref, cv_ref,
                     xo_ref, os_ref,
                     kbuf, vbuf, sem, h2_s, mlp_s, qbd_s, knp_s, vnp_s, *, layer, lam_init):
    i = pl.program_id(0)
    n_steps = pl.num_programs(0)

    def page_copies(seq, group, slot):
        copies = []
        for j in range(PAGE_GROUP):
            page = pt_ref[seq, group * PAGE_GROUP + j]
            copies.append(pltpu.make_async_copy(ck_ref.at[layer, page], kbuf.at[slot, j], sem.at[0, slot]))
            copies.append(pltpu.make_async_copy(cv_ref.at[layer, page], vbuf.at[slot, j], sem.at[1, slot]))
        return copies

    @pl.when(i == 0)
    def _():
        for slot in range(N_SLOTS):
            for c in page_copies(0, slot, slot):
                c.start()

    x1, h2 = _residual_and_mlp_input(x_ref, yp_ref, yr_ref, o_ref, mod_ref, wout_ref, gmlp_ref)
    xo_ref[...] = x1
    h2_s[...] = h2
    mlp_s[...] = jnp.zeros(mlp_s.shape, F32)
    qbd_s[...] = _sample_query_rows(q_ref[...])

    def page_group(slot, state):
        kt = jnp.concatenate([kbuf[slot, j].reshape(QK_WIDTH, PAGE_SIZE).astype(BF16)
                              for j in range(PAGE_GROUP)], axis=1)
        s = jnp.dot(qbd_s[...], kt, preferred_element_type=F32)
        v_heads = [jnp.concatenate([vbuf[slot, j, pl.ds(h, PAGE_SIZE, stride=N_HEADS), :].astype(BF16)
                                    for j in range(PAGE_GROUP)], axis=0) for h in range(N_HEADS)]
        return _sample_softmax_update(state, s, v_heads)

    def round_of_slots(r, state):
        last = r == N_ROUNDS - 1
        for slot in range(N_SLOTS):
            group = N_SLOTS * r + slot
            for c in page_copies(i, group, slot):
                c.wait()
            if slot % GROUPS_PER_FF_CHUNK == 0:
                mlp_s[...] += _mlp_chunk(h2_s[...], wup_ref, wdown_ref, group // GROUPS_PER_FF_CHUNK)
            state = page_group(slot, state)
            nxt_seq = jnp.where(last, i + 1, i)
            nxt_group = jnp.where(last, slot, group + N_SLOTS)

            @pl.when(jnp.logical_or(jnp.logical_not(last), i + 1 < n_steps))
            def _():
                for c in page_copies(nxt_seq, nxt_group, slot):
                    c.start()
        return state

    state0 = (jnp.full((N_SROWS, 128), NEG_BIG, F32), jnp.zeros((N_SROWS, 128), F32),
              jnp.zeros((N_SROWS, V_DIM), F32))
    state = lax.fori_loop(0, N_ROUNDS, round_of_slots, state0)

    _sample_new_rows_and_finish(state, qbd_s[...], kn_ref, vn_ref, knp_s, vnp_s, lamp_ref, subln_ref, os_ref, lam_init)
    xo_ref[...] = xo_ref[...] + mod_ref[:, 5 * D_MODEL:6 * D_MODEL] * mlp_s[...]


def _out_attn_call(page_table, x, yp, yr, o, mod3, wout_bf, gmlp, wup3, wdown3,
                   q_bm, k_bm, v_bm, lamp, subln, cache_kt, cache_vr, *, tm, tiles_per_seq, layer, lam_init):
    m = x.shape[0]
    assert m // tm == DEC_BATCH, "one prompt row tile per sample sequence"
    row = lambda w: pl.BlockSpec((tm, w), lambda i, pt: (i, 0))
    full = lambda a: pl.BlockSpec(a.shape, lambda i, pt: (0,) * a.ndim, pipeline_mode=pl.Buffered(1))
    tok = pl.BlockSpec((None, DEC_SEQ, QK_WIDTH), lambda i, pt: (i, 0, 0))
    hbm = pl.BlockSpec(memory_space=pl.ANY)
    grid_spec = pltpu.PrefetchScalarGridSpec(
        num_scalar_prefetch=1,
        grid=(m // tm,),
        in_specs=[row(D_MODEL), row(256), row(256), row(512),
                  pl.BlockSpec((None, 1, 6 * D_MODEL), lambda i, pt: (i // tiles_per_seq, 0, 0)),
                  full(wout_bf), full(gmlp), full(wup3), full(wdown3),
                  tok, tok, tok, full(lamp), full(subln), hbm, hbm],
        out_specs=[row(D_MODEL), tok],
        scratch_shapes=[pltpu.VMEM((N_SLOTS, PAGE_GROUP, N_HEADS, 2, QK_DIM, PAGE_SIZE), F32),
                        pltpu.VMEM((N_SLOTS, PAGE_GROUP, PAGE_SIZE * N_HEADS, V_DIM), F32),
                        pltpu.SemaphoreType.DMA((2, N_SLOTS)),
                        pltpu.VMEM((tm, D_MODEL), BF16),
                        pltpu.VMEM((tm, D_MODEL), F32),
                        pltpu.VMEM((N_SROWS, QK_WIDTH), BF16),
                        pltpu.VMEM((PAGE_SIZE, QK_WIDTH), F32),
                        pltpu.VMEM((PAGE_SIZE, ATTN_WIDTH), F32)])
    return pl.pallas_call(
        functools.partial(_out_attn_kernel, layer=layer, lam_init=lam_init),
        grid_spec=grid_spec,
        out_shape=[jax.ShapeDtypeStruct((m, D_MODEL), F32),
                   jax.ShapeDtypeStruct((DEC_BATCH, DEC_SEQ, ATTN_WIDTH), F32)],
        compiler_params=_params(1),
        name="out_attn",
    )(page_table, x, yp, yr, o, mod3, wout_bf, gmlp, wup3, wdown3, q_bm, k_bm, v_bm, lamp, subln,
      cache_kt, cache_vr)


def _block_diag(w):
    g, c, d = w.shape
    eye = jnp.eye(g, dtype=w.dtype)
    return (eye[:, None, :, None] * w[:, :, None, :]).reshape(g * c, g * d)


def _rope_tables(pos):
    half = ROPE_DIM // 2
    freqs = ROPE_THETA ** (-np.arange(half, dtype=np.float64) / half)
    ang = np.asarray(pos, np.float64)[:, None] * freqs[None, :]
    cos, sin = np.cos(ang), np.sin(ang)
    n = ang.shape[0]
    c64 = np.concatenate([cos, cos, np.ones((n, QK_DIM - ROPE_DIM))], axis=1)
    s64 = np.concatenate([-sin, sin, np.zeros((n, QK_DIM - ROPE_DIM))], axis=1)
    return (jnp.asarray(np.concatenate([c64, c64], axis=1), F32),
            jnp.asarray(np.concatenate([s64, s64], axis=1), F32))


def kernel(x_prompt, x_sample, cache_k, cache_v, page_table, state_pool, state_conv, state_rglru,
           c_prompt, c_sample, w_ada, b_ada, g_mix, w_in, pool_w, pool_scale, conv_w, conv_b,
           rg_wa, rg_ba, rg_wx, rg_bx, rg_lambda, q_norm, k_norm, lambda_q1, lambda_k1,
           lambda_q2, lambda_k2, subln, w_out, g_mlp, w_up, w_down):
    n_pool = cache_k.shape[1]
    mp = BATCH * SEQ
    ms = DEC_BATCH * DEC_SEQ
    tm_tiles = SEQ // PROJ_TM

    cos_p, sin_p = _rope_tables(np.arange(SEQ))
    cos_s, sin_s = _rope_tables(PAST_LEN + np.repeat(np.arange(DEC_SEQ), DEC_BATCH))
    grp = _block_diag(jnp.ones((QK_WIDTH // QK_DIM, QK_DIM, QK_DIM), BF16))
    cache_kt = cache_k.transpose(0, 1, 3, 4, 5, 2)
    cache_vr = cache_v.reshape(DEPTH, n_pool, PAGE_SIZE * N_HEADS, V_DIM)
    pad_rows = (-(BATCH + DEC_BATCH)) % 16
    c_all = jnp.concatenate([c_prompt, c_sample, jnp.zeros((pad_rows, D_MODEL), F32)], axis=0)

    def to_tm(a):
        return a.transpose(1, 0, 2).reshape(ms, a.shape[-1])

    def to_bm(a):
        return a.reshape(DEC_SEQ, DEC_BATCH, a.shape[-1]).transpose(1, 0, 2)

    xp_rows = x_prompt.reshape(mp, D_MODEL)
    xs_rows = to_tm(x_sample)
    outs = [[] for _ in range(10)]
    kv_all = None
    for l in range(DEPTH):
        lam_init = 0.8 - 0.6 * math.exp(-0.3 * l)
        mod = _ada_call(c_all, w_ada, b_ada, layer=l)
        mod_p = mod[0:BATCH].reshape(BATCH, 1, 6 * D_MODEL)
        mod_s = jnp.tile(mod[BATCH:BATCH + DEC_BATCH], (DEC_SEQ, 1)).reshape(1, ms, 6 * D_MODEL)
        gmix = g_mix[l].reshape(1, D_MODEL)
        gmlp = g_mlp[l].reshape(1, D_MODEL)
        w_in_bf = w_in[l].astype(BF16)
        wout_bf = w_out[l].astype(BF16)
        wup3 = w_up[l].astype(BF16).reshape(D_MODEL, N_FF_CHUNKS, FF_CHUNK).transpose(1, 0, 2)
        wdown3 = w_down[l].astype(BF16).reshape(N_FF_CHUNKS, FF_CHUNK, D_MODEL)
        qg = jnp.tile(q_norm[l], QK_WIDTH // QK_DIM).reshape(1, QK_WIDTH)
        kg = jnp.tile(k_norm[l], QK_WIDTH // QK_DIM).reshape(1, QK_WIDTH)
        mix_wts = (_block_diag(pool_w[l]).astype(BF16), pool_scale[l].reshape(1, -1),
                   conv_w[l], conv_b[l].reshape(1, -1),
                   jnp.concatenate([_block_diag(rg_wa[l]), _block_diag(rg_wx[l])], axis=1).astype(BF16),
                   jnp.concatenate([rg_ba[l], rg_bx[l]]).reshape(1, -1),
                   rg_lambda[l].reshape(1, -1))
        lamp = jnp.stack([lambda_q1[l], lambda_k1[l], lambda_q2[l], lambda_k2[l]])
        sub_g = subln[l].reshape(1, V_DIM)

        q, kt_all, v_all, ypool, yrec, hlast, pstate, cstate = _proj_mix_call(
            xp_rows, mod_p, gmix, w_in_bf, cos_p, sin_p, qg, kg, grp, mix_wts, kv_all,
            tm=PROJ_TM, tiles_per_seq=tm_tiles, layer=l, n_layers=DEPTH)
        kv_all = (kt_all, v_all)
        o = _attn_prompt_call(q.reshape(BATCH, SEQ, QK_WIDTH), kt_all,
                              v_all.reshape(DEPTH, BATCH, SEQ * N_HEADS, V_DIM), lamp, sub_g,
                              layer=l, lam_init=lam_init)
        outs[4].append(pstate[:, 16 - POOL_HIST:])
        outs[6].append(cstate[:, 8 - (CONV_WIDTH - 1):])
        outs[8].append(hlast.reshape(BATCH, REC_WIDTH))

        xp_s, xr_s, gate_s, q_s, kt_s, v_s = _proj_call(xs_rows, mod_s, gmix, w_in_bf, cos_s, sin_s, qg, kg, grp,
                                                         tm=ms, tiles_per_seq=1)
        k_s = kt_s[0, 0].T
        hist = (state_pool[l].transpose(1, 0, 2).reshape(POOL_HIST * DEC_BATCH, POOL_WIDTH),
                state_conv[l].transpose(1, 0, 2).reshape((CONV_WIDTH - 1) * DEC_BATCH, REC_WIDTH),
                state_rglru[l])
        ypool_s, yrec_s, hlast_s = _mix_call(xp_s, xr_s, gate_s, hist, mix_wts, n_seq=1, n_t=1,
                                             stride=DEC_BATCH, rows=ms, pos0=PAST_LEN)
        k_bm, v_bm = to_bm(k_s), to_bm(v_s.reshape(ms, ATTN_WIDTH))

        xp_rows, o_s = _out_attn_call(page_table, xp_rows, ypool, yrec, o.reshape(mp, ATTN_WIDTH), mod_p, wout_bf,
                                      gmlp, wup3, wdown3, to_bm(q_s.astype(F32)), k_bm, v_bm, lamp, sub_g,
                                      cache_kt, cache_vr, tm=PROJ_TM, tiles_per_seq=tm_tiles, layer=l,
                                      lam_init=lam_init)
        xs_rows = _out_call(xs_rows, ypool_s, yrec_s, to_tm(o_s), mod_s, wout_bf, gmlp,
                            wup3, wdown3, tm=ms, tiles_per_seq=1)
        outs[2].append(k_bm.reshape(DEC_BATCH, DEC_SEQ, N_HEADS, 2, QK_DIM))
        outs[3].append(v_bm.reshape(DEC_BATCH, DEC_SEQ, N_HEADS, V_DIM))
        outs[5].append(jnp.concatenate([state_pool[l][:, DEC_SEQ:], to_bm(xp_s)], axis=1))
        outs[7].append(to_bm(xr_s)[:, DEC_SEQ - (CONV_WIDTH - 1):])
        outs[9].append(hlast_s.reshape(DEC_BATCH, REC_WIDTH))

    kt_all, v_all = kv_all
    new_k_prompt = kt_all.reshape(DEPTH, BATCH, N_HEADS, 2, QK_DIM, SEQ).transpose(0, 1, 5, 2, 3, 4)
    new_v_prompt = v_all.reshape(DEPTH, BATCH, SEQ, N_HEADS, V_DIM)
    stacked = [jnp.stack(o) for o in outs[2:]]
    return (xp_rows.reshape(BATCH, SEQ, D_MODEL), to_bm(xs_rows), new_k_prompt, new_v_prompt) + tuple(stacked)
```

```python
import functools
import math

import jax
import jax.numpy as jnp
import numpy as np
from jax import lax
from jax.experimental import pallas as pl
from jax.experimental.pallas import tpu as pltpu

D_MODEL = 1024
BATCH = 4
SEQ = 4096
DEPTH = 2
DEC_BATCH = 32
DEC_SEQ = 4
PAST_LEN = 16384
PAGE_SIZE = 128
N_PAGES = PAST_LEN // PAGE_SIZE

POOL_WIDTH = 256
POOL_WINDOWS = (2, 4, 8, 16)
POOL_GROUP = 64
POOL_HIST = 15
REC_WIDTH = 256
CONV_WIDTH = 4
RGLRU_C = 8.0
N_HEADS = 4
QK_DIM = 64
V_DIM = 128
ATTN_WIDTH = N_HEADS * V_DIM
ROPE_DIM = 16
ROPE_THETA = 500000.0
ATTN_SCALE = 1.0 / math.sqrt(QK_DIM)
Q_PRESCALE = ATTN_SCALE * math.log2(math.e)
QK_WIDTH = N_HEADS * 2 * QK_DIM
IN_WIDTH = POOL_WIDTH + 2 * REC_WIDTH + 2 * QK_WIDTH + ATTN_WIDTH
D_FF = 4 * D_MODEL
EPS = 1e-6

F32 = jnp.float32
BF16 = jnp.bfloat16
NEG_BIG = -1e30
V7X_VMEM_LIMIT = 56 * 1024 * 1024

PROJ_TM = 512
OUT_TM = BATCH * SEQ // DEC_BATCH
ATTN_TQ = 4096
ATTN_ROW_CHUNK = 256
ATTN_DIAG_CHUNK = 256
PAGE_GROUP = 8
N_GROUPS = N_PAGES // PAGE_GROUP
N_SLOTS = 4
GROUPS_PER_FF_CHUNK = 2
N_FF_CHUNKS = N_GROUPS // GROUPS_PER_FF_CHUNK
FF_CHUNK = D_FF // N_FF_CHUNKS


def _bdot(a, b):
    return jnp.dot(a.astype(BF16), b.astype(BF16), preferred_element_type=F32)


def _params(n_axes):
    return pltpu.CompilerParams(dimension_semantics=("arbitrary",) * n_axes,
                                vmem_limit_bytes=V7X_VMEM_LIMIT)


def _ada_kernel(c_ref, w_ref, b_ref, o_ref):
    c = c_ref[...]
    o_ref[...] = _bdot(c * jax.nn.sigmoid(c), w_ref[...]) + b_ref[...]


def _ada_call(c_all, w_ada, b_ada, *, layer):
    rows = c_all.shape[0]
    tn = 1536
    return pl.pallas_call(
        _ada_kernel,
        grid=(6 * D_MODEL // tn,),
        in_specs=[pl.BlockSpec((rows, D_MODEL), lambda j: (0, 0)),
                  pl.BlockSpec((None, D_MODEL, tn), lambda j: (layer, 0, j)),
                  pl.BlockSpec((None, 1, tn), lambda j: (layer, 0, j))],
        out_specs=pl.BlockSpec((rows, tn), lambda j: (0, j)),
        out_shape=jax.ShapeDtypeStruct((rows, 6 * D_MODEL), F32),
        compiler_params=_params(1),
        name="ada",
    )(c_all, w_ada, b_ada.reshape(DEPTH, 1, 6 * D_MODEL))


def _proj_body(x_ref, mod_ref, gmix_ref, w_ref, cos_ref, sin_ref, qg_ref, kg_ref, grp_ref,
               xp_ref, xr_ref, gate_ref, q_ref, kt_ref, v_ref):
    x = x_ref[...]
    y = x * lax.rsqrt(jnp.mean(x * x, axis=-1, keepdims=True) + EPS) * gmix_ref[...]
    sh1 = mod_ref[:, 0:D_MODEL]
    sc1 = mod_ref[:, D_MODEL:2 * D_MODEL]
    h = y * (1.0 + sc1) + sh1
    proj = _bdot(h, w_ref[...])
    xp_ref[...] = proj[:, 0:256]
    xr_ref[...] = proj[:, 256:512]
    gate_ref[...] = proj[:, 512:768]
    tm = x.shape[0]
    for slot in range(v_ref.shape[0]):
        for hd in range(N_HEADS):
            v_ref[slot, pl.ds(hd, tm, stride=N_HEADS), :] = proj[:, 1792 + hd * V_DIM:1792 + (hd + 1) * V_DIM]

    cos = jnp.concatenate([cos_ref[...]] * N_HEADS, axis=1)
    sin = jnp.concatenate([sin_ref[...]] * N_HEADS, axis=1)
    lane = lax.broadcasted_iota(jnp.int32, cos.shape, 1) % QK_DIM
    grp = grp_ref[...]

    def norm_rope(z, g):
        ssq = _bdot(z * z, grp)
        zn = z * lax.rsqrt(ssq * (1.0 / QK_DIM) + EPS) * g
        half = ROPE_DIM // 2
        swapped = jnp.where(lane < half, pltpu.roll(zn, QK_WIDTH - half, 1), pltpu.roll(zn, half, 1))
        return zn * cos + swapped * sin

    q = norm_rope(proj[:, 768:1280], qg_ref[...])
    q_ref[...] = (q * Q_PRESCALE).astype(q_ref.dtype)
    kt = norm_rope(proj[:, 1280:1792], kg_ref[...]).T
    for slot in range(kt_ref.shape[0]):
        kt_ref[slot] = kt


def _proj_kernel(*refs):
    _proj_body(*refs)


def _proj_io(x, mod3, gmix, w_in_bf, cos_t, sin_t, qg, kg, grp, *, tm, tiles_per_seq):
    mod_rows = mod3.shape[1]
    full = lambda a: pl.BlockSpec(a.shape, lambda i: (0,) * a.ndim)
    ins = [x, mod3, gmix, w_in_bf, cos_t, sin_t, qg, kg, grp]
    in_specs = [pl.BlockSpec((tm, D_MODEL), lambda i: (i, 0)),
                pl.BlockSpec((None, mod_rows, 6 * D_MODEL), lambda i: (i // tiles_per_seq, 0, 0)),
                full(gmix), full(w_in_bf),
                pl.BlockSpec((tm, V_DIM), lambda i: (i % tiles_per_seq, 0)),
                pl.BlockSpec((tm, V_DIM), lambda i: (i % tiles_per_seq, 0)),
                full(qg), full(kg), full(grp)]
    return ins, in_specs


def _kv_out(m, tm, tiles_per_seq, layer, n_layers):
    n_seq = m // (tm * tiles_per_seq)
    seq_len = tm * tiles_per_seq
    slots, first = (n_layers, 0) if layer == 0 else (1, layer)
    specs = [pl.BlockSpec((slots, None, QK_WIDTH, tm), lambda i: (first, i // tiles_per_seq, 0, i % tiles_per_seq)),
             pl.BlockSpec((slots, tm * N_HEADS, V_DIM), lambda i: (first, i, 0))]
    shapes = [jax.ShapeDtypeStruct((n_layers, n_seq, QK_WIDTH, seq_len), F32),
              jax.ShapeDtypeStruct((n_layers, m * N_HEADS, V_DIM), F32)]
    return specs, shapes


def _proj_call(x, mod3, gmix, w_in_bf, cos_t, sin_t, qg, kg, grp, *, tm, tiles_per_seq):
    m = x.shape[0]
    ins, in_specs = _proj_io(x, mod3, gmix, w_in_bf, cos_t, sin_t, qg, kg, grp, tm=tm, tiles_per_seq=tiles_per_seq)
    kv_specs, kv_shapes = _kv_out(m, tm, tiles_per_seq, 0, 1)
    row = lambda w: pl.BlockSpec((tm, w), lambda i: (i, 0))
    return pl.pallas_call(
        _proj_kernel,
        grid=(m // tm,),
        in_specs=in_specs,
        out_specs=[row(256), row(256), row(256), row(512)] + kv_specs,
        out_shape=[jax.ShapeDtypeStruct((m, 256), F32)] * 3 + [jax.ShapeDtypeStruct((m, 512), BF16)] + kv_shapes,
        compiler_params=_params(1),
        name="proj",
    )(*ins)


def _mix_kernel(*refs, stride, rows, pos0, has_hist):
    hist = refs[3:6] if has_hist else None
    rest = refs[6:] if has_hist else refs[3:]
    t = pl.program_id(1)
    _mix_prepare(t, hist, *rest[-3:], stride=stride, rows=rows)
    _mix_body(t, refs[0], refs[1], refs[2], *rest, stride=stride, rows=rows, pos0=pos0)


def _mix_prepare(t, hist, pext, cext, hcar, *, stride, rows):
    has_hist = hist is not None
    if has_hist:
        hp_ref, hc_ref, h0_ref = hist
    hp_rows = 16 * stride
    hc_rows = 8 * stride

    @pl.when(t == 0)
    def _():
        pext[0:hp_rows, :] = jnp.zeros((hp_rows, POOL_WIDTH), F32)
        cext[0:hc_rows, :] = jnp.zeros((hc_rows, REC_WIDTH), F32)
        if has_hist:
            pext[stride:hp_rows, :] = hp_ref[...]
            cext[(8 - (CONV_WIDTH - 1)) * stride:hc_rows, :] = hc_ref[...]
            hcar[...] = h0_ref[...]
        else:
            hcar[...] = jnp.zeros((stride, REC_WIDTH), F32)

    @pl.when(t > 0)
    def _():
        pext[0:hp_rows, :] = pext[rows:rows + hp_rows, :]
        cext[0:hc_rows, :] = cext[rows:rows + hc_rows, :]


def _mix_body(t, xp_ref, xr_ref, gate_ref, poolw_ref, pscale_ref, convw_ref, convb_ref,
              wax_ref, bax_ref, lam_ref, ypool_ref, yrec_ref, hlast_ref, pext, cext, hcar, *, stride, rows, pos0,
              scan_scratch=None):
    tt = rows // stride
    hp_rows = 16 * stride
    hc_rows = 8 * stride
    pext[hp_rows:hp_rows + rows, :] = xp_ref[...]
    cext[hc_rows:hc_rows + rows, :] = xr_ref[...]

    tidx = lax.broadcasted_iota(jnp.int32, (rows, 128), 0) // stride
    lane = lax.broadcasted_iota(jnp.int32, (rows, 128), 1)
    pos1 = (pos0 + 1 + t * tt + tidx).astype(F32)

    def back(j, lo):
        return pext[hp_rows - j * stride:hp_rows - j * stride + rows, lo:lo + 128]

    def window_sums(lo, w_small):
        acc = back(0, lo)
        for j in range(1, w_small):
            acc = acc + back(j, lo)
        big = acc
        for j in range(w_small, 2 * w_small):
            big = big + back(j, lo)
        cnt = jnp.where(lane < POOL_GROUP, jnp.minimum(float(w_small), pos1),
                        jnp.minimum(float(2 * w_small), pos1))
        return jnp.where(lane < POOL_GROUP, acc, big) / cnt - back(0, lo)

    d = jnp.concatenate([window_sums(0, 2), window_sums(128, 8)], axis=1)
    ypool_ref[...] = (_bdot(d, poolw_ref[...]) * pscale_ref[...]).astype(ypool_ref.dtype)

    y = convb_ref[...]
    for tap in range(CONV_WIDTH):
        off = hc_rows - (CONV_WIDTH - 1 - tap) * stride
        y = y + cext[off:off + rows, :] * convw_ref[tap:tap + 1, :]
    rg = _bdot(y, wax_ref[...]) + bax_ref[...]
    r = jax.nn.sigmoid(rg[:, 0:REC_WIDTH])
    i = jax.nn.sigmoid(rg[:, REC_WIDTH:2 * REC_WIDTH])
    nl = -lam_ref[...]
    softplus = jnp.maximum(nl, 0.0) + jnp.log1p(jnp.exp(-jnp.abs(nl)))
    a = jnp.exp(-RGLRU_C * r * softplus)
    b = jnp.sqrt(1.0 - a * a) * (i * y)

    def doubling_scan(a, b, idx, n, shift):
        step = 1
        while step < n:
            a_prev = pltpu.roll(a, step * shift, 0)
            b_prev = pltpu.roll(b, step * shift, 0)
            valid = idx >= step
            b = jnp.where(valid, a * b_prev + b, b)
            a = jnp.where(valid, a * a_prev, a)
            step *= 2
        return a, b

    hc = hcar[...]
    row = lax.broadcasted_iota(jnp.int32, (rows, REC_WIDTH), 0)
    if scan_scratch is None:
        a, b = doubling_scan(a, b, row // stride, tt, stride)
        carry = jnp.broadcast_to(hc, (rows, REC_WIDTH)) if stride == 1 else jnp.concatenate([hc] * tt, axis=0)
    else:
        a_s, b_s, hin_s = scan_scratch
        n_groups = rows // 8
        a, b = doubling_scan(a, b, row % 8, 8, 1)
        halves = range(REC_WIDTH // 128)

        def put(ref, val):
            for k in halves:
                ref[k] = val[:, 128 * k:128 * (k + 1)]

        put(a_s, a)
        put(b_s, b)
        ga = jnp.concatenate([a_s[k, pl.ds(7, n_groups, stride=8), :] for k in halves], axis=1)
        gb = jnp.concatenate([b_s[k, pl.ds(7, n_groups, stride=8), :] for k in halves], axis=1)
        gidx = lax.broadcasted_iota(jnp.int32, (n_groups, REC_WIDTH), 0)
        ga, gb = doubling_scan(ga, gb, gidx, n_groups, 1)
        after = gb + ga * hc
        put(hin_s, jnp.where(gidx == 0, jnp.broadcast_to(hc, after.shape), pltpu.roll(after, 1, 0)))
        carry = jnp.concatenate(
            [jnp.concatenate([hin_s[k, pl.ds(g, 8, stride=0), :] for g in range(n_groups)], axis=0) for k in halves],
            axis=1)
    h = b + a * carry
    h_end = h[rows - stride:rows, :]
    hcar[...] = h_end
    hlast_ref[...] = h_end

    g = gate_ref[...]
    gelu = 0.5 * g * (1.0 + jnp.tanh(math.sqrt(2.0 / math.pi) * (g + 0.044715 * (g * g * g))))
    yrec_ref[...] = (h * gelu).astype(yrec_ref.dtype)


def _mix_call(xp, xr, gate, hist, wts, *, n_seq, n_t, stride, rows, pos0):
    m = xp.shape[0]
    has_hist = hist is not None
    row = pl.BlockSpec((rows, 256), lambda b, t: (b * n_t + t, 0))
    full = lambda a: pl.BlockSpec(a.shape, lambda b, t: (0,) * a.ndim)
    ins = [xp, xr, gate] + (list(hist) if has_hist else []) + list(wts)
    in_specs = [row, row, row] + [full(a) for a in ins[3:]]
    return pl.pallas_call(
        functools.partial(_mix_kernel, stride=stride, rows=rows, pos0=pos0, has_hist=has_hist),
        grid=(n_seq, n_t),
        in_specs=in_specs,
        out_specs=[row, row, pl.BlockSpec((None, stride, REC_WIDTH), lambda b, t: (b, 0, 0))],
        out_shape=[jax.ShapeDtypeStruct((m, 256), BF16), jax.ShapeDtypeStruct((m, 256), BF16),
                   jax.ShapeDtypeStruct((n_seq, stride, REC_WIDTH), F32)],
        scratch_shapes=[pltpu.VMEM((16 * stride + rows, POOL_WIDTH), F32),
                        pltpu.VMEM((8 * stride + rows, REC_WIDTH), F32),
                        pltpu.VMEM((stride, REC_WIDTH), F32)],
        compiler_params=_params(2),
        name="mix",
    )(*ins)


N_MIX_WTS = 7


def _proj_mix_kernel(*refs, rows, tiles_per_seq, aliased):
    proj_in = refs[0:9]
    mix_wts = refs[9:9 + N_MIX_WTS]
    outs = refs[9 + N_MIX_WTS + (2 if aliased else 0):]
    q_ref, kt_ref, v_ref, ypool_ref, yrec_ref, hlast_ref, pstate_ref, cstate_ref = outs[0:8]
    xp_s, xr_s, gate_s, pext, cext, hcar, scan_a, scan_b, scan_h = outs[8:]
    t = pl.program_id(0) % tiles_per_seq
    _mix_prepare(t, None, pext, cext, hcar, stride=1, rows=rows)
    _proj_body(*proj_in, xp_s, xr_s, gate_s, q_ref, kt_ref, v_ref)
    _mix_body(t, xp_s, xr_s, gate_s, *mix_wts, ypool_ref, yrec_ref, hlast_ref, pext, cext, hcar,
              stride=1, rows=rows, pos0=0, scan_scratch=(scan_a, scan_b, scan_h))
    pstate_ref[...] = pext[rows:rows + 16, :]
    cstate_ref[...] = cext[rows:rows + 8, :]


def _proj_mix_call(x, mod3, gmix, w_in_bf, cos_t, sin_t, qg, kg, grp, mix_wts, kv_prev, *,
                   tm, tiles_per_seq, layer, n_layers):
    m = x.shape[0]
    n_seq = m // (tm * tiles_per_seq)
    ins, in_specs = _proj_io(x, mod3, gmix, w_in_bf, cos_t, sin_t, qg, kg, grp, tm=tm, tiles_per_seq=tiles_per_seq)
    full = lambda a: pl.BlockSpec(a.shape, lambda i: (0,) * a.ndim)
    ins += list(mix_wts)
    in_specs += [full(a) for a in mix_wts]
    aliases = {}
    if kv_prev is not None:
        aliases = {len(ins): 1, len(ins) + 1: 2}
        ins += list(kv_prev)
        in_specs += [pl.BlockSpec(memory_space=pl.ANY)] * 2
    kv_specs, kv_shapes = _kv_out(m, tm, tiles_per_seq, layer, n_layers)
    row = lambda w: pl.BlockSpec((tm, w), lambda i: (i, 0))
    per_seq = lambda r: pl.BlockSpec((None, r, 256), lambda i: (i // tiles_per_seq, 0, 0))
    return pl.pallas_call(
        functools.partial(_proj_mix_kernel, rows=tm, tiles_per_seq=tiles_per_seq, aliased=kv_prev is not None),
        grid=(m // tm,),
        in_specs=in_specs,
        out_specs=[row(512)] + kv_specs + [row(256), row(256), per_seq(1), per_seq(16), per_seq(8)],
        out_shape=[jax.ShapeDtypeStruct((m, 512), BF16)] + kv_shapes
                  + [jax.ShapeDtypeStruct((m, 256), BF16), jax.ShapeDtypeStruct((m, 256), BF16),
                     jax.ShapeDtypeStruct((n_seq, 1, REC_WIDTH), F32),
                     jax.ShapeDtypeStruct((n_seq, 16, POOL_WIDTH), F32),
                     jax.ShapeDtypeStruct((n_seq, 8, REC_WIDTH), F32)],
        scratch_shapes=[pltpu.VMEM((tm, POOL_WIDTH), F32), pltpu.VMEM((tm, REC_WIDTH), F32),
                        pltpu.VMEM((tm, REC_WIDTH), F32),
                        pltpu.VMEM((16 + tm, POOL_WIDTH), F32), pltpu.VMEM((8 + tm, REC_WIDTH), F32),
                        pltpu.VMEM((1, REC_WIDTH), F32),
                        pltpu.VMEM((REC_WIDTH // 128, tm, 128), F32), pltpu.VMEM((REC_WIDTH // 128, tm, 128), F32),
                        pltpu.VMEM((REC_WIDTH // 128, tm // 8, 128), F32)],
        input_output_aliases=aliases,
        compiler_params=_params(1),
        name="proj_mix",
    )(*ins)


def _diff_lambda(lamp_ref, lam_init):
    lp = lamp_ref[...]
    s1 = jnp.sum(lp[0:1, :] * lp[1:2, :], axis=-1, keepdims=True)
    s2 = jnp.sum(lp[2:3, :] * lp[3:4, :], axis=-1, keepdims=True)
    return jnp.exp(s1) - jnp.exp(s2) + lam_init


def _subln(o, g, lam_init):
    return o * lax.rsqrt(jnp.mean(o * o, axis=-1, keepdims=True) + EPS) * g * (1.0 - lam_init)


def _attn_prompt_kernel(q_ref, k_ref, v_ref, lamp_ref, subln_ref, o_ref, kb, vb, q2, m_s, acc_s, *,
                        tq, full_chunk, diag_chunk, lam_init):
    qi = pl.program_id(2)

    @pl.when(qi == 0)
    def _():
        for j in range(SEQ // tq):
            kb[j] = k_ref[:, j * tq:(j + 1) * tq].astype(BF16)
        head = pl.program_id(1)
        vb[:, 0:V_DIM] = v_ref[pl.ds(head, SEQ, stride=N_HEADS), :].astype(BF16)
        vb[:, V_DIM:2 * V_DIM] = jnp.ones((SEQ, V_DIM), BF16)

    q = q_ref[...]
    lane = lax.broadcasted_iota(jnp.int32, (tq, 2 * QK_DIM), 1)
    zero = jnp.zeros_like(q)
    q2[0:tq, :] = jnp.where(lane < QK_DIM, q, zero)
    q2[tq:2 * tq, :] = jnp.where(lane >= QK_DIM, q, zero)
    m_s[...] = jnp.full((2 * tq, 128), NEG_BIG, F32)
    acc_s[...] = jnp.zeros((2 * tq, 2 * V_DIM), F32)

    def step(kj, diagonal):
        start = pl.multiple_of(kj * tq, tq)
        kt = kb[kj]
        chunk = diag_chunk if diagonal else full_chunk
        for c0 in range(0, 2 * tq, chunk):
            rows = slice(c0, c0 + chunk)
            nk = (c0 % tq) + chunk if diagonal else tq
            s = jnp.dot(q2[rows, :], kt[:, 0:nk], preferred_element_type=F32)
            if diagonal:
                r = lax.broadcasted_iota(jnp.int32, (chunk, chunk), 0)
                c = lax.broadcasted_iota(jnp.int32, (chunk, chunk), 1)
                tail = jnp.where(c <= r, s[:, nk - chunk:nk], NEG_BIG)
                s = tail if nk == chunk else jnp.concatenate([s[:, 0:nk - chunk], tail], axis=1)
            m_old = m_s[rows, :]
            m_new = jnp.maximum(m_old, jnp.max(s, axis=-1, keepdims=True))
            e = jnp.concatenate([jnp.exp2(s[:, j:j + 128] - m_new) for j in range(0, nk, 128)], axis=1)
            pv = jnp.dot(e.astype(BF16), vb[pl.ds(start, nk), :], preferred_element_type=F32)
            alpha = jnp.exp2(m_old - m_new)
            acc_s[rows, :] = jnp.concatenate([alpha, alpha], axis=1) * acc_s[rows, :] + pv
            m_s[rows, :] = m_new

    def body(kj, carry):
        step(kj, False)
        return carry

    lax.fori_loop(0, qi, body, 0)
    step(qi, True)

    a0 = acc_s[0:tq, :]
    a1 = acc_s[tq:2 * tq, :]
    lam = _diff_lambda(lamp_ref, lam_init)
    o = a0[:, 0:V_DIM] / a0[:, V_DIM:2 * V_DIM] - lam * (a1[:, 0:V_DIM] / a1[:, V_DIM:2 * V_DIM])
    o_ref[...] = _subln(o, subln_ref[...], lam_init).astype(o_ref.dtype)


def _attn_prompt_call(q, kt_all, v_all, lamp, subln, *, layer, lam_init):
    tq = ATTN_TQ
    qspec = pl.BlockSpec((None, tq, V_DIM), lambda b, h, i: (b, i, h))
    full = lambda a: pl.BlockSpec(a.shape, lambda b, h, i: (0,) * a.ndim)
    return pl.pallas_call(
        functools.partial(_attn_prompt_kernel, tq=tq, full_chunk=ATTN_ROW_CHUNK, diag_chunk=ATTN_DIAG_CHUNK,
                          lam_init=lam_init),
        grid=(BATCH, N_HEADS, SEQ // tq),
        in_specs=[qspec,
                  pl.BlockSpec((None, None, 2 * QK_DIM, SEQ), lambda b, h, i: (layer, b, h, 0)),
                  pl.BlockSpec((None, None, SEQ * N_HEADS, V_DIM), lambda b, h, i: (layer, b, 0, 0)),
                  full(lamp), full(subln)],
        out_specs=qspec,
        out_shape=jax.ShapeDtypeStruct((BATCH, SEQ, ATTN_WIDTH), BF16),
        scratch_shapes=[pltpu.VMEM((SEQ // tq, 2 * QK_DIM, tq), BF16),
                        pltpu.VMEM((SEQ, 2 * V_DIM), BF16),
                        pltpu.VMEM((2 * tq, 2 * QK_DIM), BF16),
                        pltpu.VMEM((2 * tq, 128), F32),
                        pltpu.VMEM((2 * tq, 2 * V_DIM), F32)],
        compiler_params=_params(3),
        name="attn_prompt",
    )(q, kt_all, v_all, lamp, subln)


N_SROWS = DEC_SEQ * N_HEADS * 2


def _sample_query_rows(q):
    r = lax.broadcasted_iota(jnp.int32, (N_SROWS, QK_WIDTH), 0)
    g = lax.broadcasted_iota(jnp.int32, (N_SROWS, QK_WIDTH), 1) // QK_DIM
    rows = jnp.zeros((N_SROWS, QK_WIDTH), F32)
    for t in range(DEC_SEQ):
        rows = jnp.where(r % DEC_SEQ == t, jnp.broadcast_to(q[t:t + 1, :], (N_SROWS, QK_WIDTH)), rows)
    return jnp.where(g == r // DEC_SEQ, rows, 0.0).astype(BF16)


def _sample_softmax_update(state, s, v_heads):
    m_old, l_old, acc_old = state
    m_new = jnp.maximum(m_old, jnp.max(s, axis=-1, keepdims=True))
    alpha = jnp.exp2(m_old - m_new)
    e = jnp.concatenate([jnp.exp2(s[:, j:j + 128] - m_new) for j in range(0, s.shape[1], 128)], axis=1)
    l_new = alpha * l_old + jnp.sum(e, axis=-1, keepdims=True)
    eb = e.astype(BF16)
    pv = [jnp.dot(eb[8 * h:8 * h + 8, :], v_heads[h], preferred_element_type=F32) for h in range(N_HEADS)]
    return m_new, l_new, alpha * acc_old + jnp.concatenate(pv, axis=0)


def _sample_new_rows_and_finish(state, qbd, kn_ref, vn_ref, knp_ref, vnp_ref, lamp_ref, subln_ref, os_ref, lam_init):
    knp_ref[...] = jnp.zeros((PAGE_SIZE, QK_WIDTH), F32)
    vnp_ref[...] = jnp.zeros((PAGE_SIZE, ATTN_WIDTH), F32)
    knp_ref[0:DEC_SEQ, :] = kn_ref[...]
    vnp_ref[0:DEC_SEQ, :] = vn_ref[...]
    knp = knp_ref[...].astype(BF16)
    vnp = vnp_ref[...].astype(BF16)
    s_new = lax.dot_general(qbd, knp, (((1,), (1,)), ((), ())), preferred_element_type=F32)
    row_t = lax.broadcasted_iota(jnp.int32, (N_SROWS, PAGE_SIZE), 0) % DEC_SEQ
    col = lax.broadcasted_iota(jnp.int32, (N_SROWS, PAGE_SIZE), 1)
    state = _sample_softmax_update(state, jnp.where(col <= row_t, s_new, NEG_BIG),
                                   [vnp[:, h * V_DIM:(h + 1) * V_DIM] for h in range(N_HEADS)])
    _, l, acc = state
    on = acc / l
    lam = _diff_lambda(lamp_ref, lam_init)
    gain = subln_ref[...]
    for h in range(N_HEADS):
        blk = on[8 * h:8 * h + 8, :]
        diff = blk - lam * pltpu.roll(blk, DEC_SEQ, 0)
        os_ref[:, h * V_DIM:(h + 1) * V_DIM] = _subln(diff, gain, lam_init)[0:DEC_SEQ, :]


def _residual_and_mlp_input(x_ref, yp_ref, yr_ref, o_ref, mod_ref, wout_ref, gmlp_ref):
    mix = (jnp.dot(yp_ref[...].astype(BF16), wout_ref[0:256, :], preferred_element_type=F32)
           + jnp.dot(yr_ref[...].astype(BF16), wout_ref[256:512, :], preferred_element_type=F32)
           + jnp.dot(o_ref[...].astype(BF16), wout_ref[512:1024, :], preferred_element_type=F32))
    g1 = mod_ref[:, 2 * D_MODEL:3 * D_MODEL]
    sh2 = mod_ref[:, 3 * D_MODEL:4 * D_MODEL]
    sc2 = mod_ref[:, 4 * D_MODEL:5 * D_MODEL]
    x1 = x_ref[...] + g1 * mix
    y = x1 * lax.rsqrt(jnp.mean(x1 * x1, axis=-1, keepdims=True) + EPS) * gmlp_ref[...]
    return x1, (y * (1.0 + sc2) + sh2).astype(BF16)


def _mlp_chunk(h2, wup_ref, wdown_ref, c):
    cols = slice(c * FF_CHUNK, (c + 1) * FF_CHUNK)
    up = jnp.dot(h2, wup_ref[:, cols], preferred_element_type=F32)
    act = jnp.square(jnp.maximum(up, 0.0)).astype(BF16)
    return jnp.dot(act, wdown_ref[cols, :], preferred_element_type=F32)


def _out_kernel(x_ref, yp_ref, yr_ref, o_ref, mod_ref, wout_ref, gmlp_ref, wup_ref, wdown_ref, xo_ref):
    x1, h2 = _residual_and_mlp_input(x_ref, yp_ref, yr_ref, o_ref, mod_ref, wout_ref, gmlp_ref)
    mlp = _mlp_chunk(h2, wup_ref, wdown_ref, 0)
    for c in range(1, N_FF_CHUNKS):
        mlp = mlp + _mlp_chunk(h2, wup_ref, wdown_ref, c)
    xo_ref[...] = x1 + mod_ref[:, 5 * D_MODEL:6 * D_MODEL] * mlp


def _out_call(x, yp, yr, o, mod3, wout_bf, gmlp, wup_bf, wdown_bf, *, tm, tiles_per_seq):
    m = x.shape[0]
    mod_rows = mod3.shape[1]
    row = lambda w: pl.BlockSpec((tm, w), lambda i: (i, 0))
    full = lambda a: pl.BlockSpec(a.shape, lambda i: (0,) * a.ndim, pipeline_mode=pl.Buffered(1))
    return pl.pallas_call(
        _out_kernel,
        grid=(m // tm,),
        in_specs=[row(D_MODEL), row(256), row(256), row(512),
                  pl.BlockSpec((None, mod_rows, 6 * D_MODEL), lambda i: (i // tiles_per_seq, 0, 0)),
                  full(wout_bf), full(gmlp), full(wup_bf), full(wdown_bf)],
        out_specs=row(D_MODEL),
        out_shape=jax.ShapeDtypeStruct((m, D_MODEL), F32),
        compiler_params=_params(1),
        name="out",
    )(x, yp, yr, o, mod3, wout_bf, gmlp, wup_bf, wdown_bf)


def _out_attn_kernel(pt_ref, x_ref, yp_ref, yr_ref, o_ref, mod_ref, wout_ref, gmlp_ref, wup_ref, wdown_ref,
                     q_ref, kn_ref, vn_ref, lamp_ref, subln_ref, ck_ref, cv_ref,
                     xo_ref, os_ref,
                     kbuf, vbuf, sem, h2_s, mlp_s, qbd_s, knp_s, vnp_s, *, layer, lam_init):
    i = pl.program_id(0)
    n_steps = pl.num_programs(0)

    def page_copies(seq, group, slot):
        copies = []
        for j in range(PAGE_GROUP):
            page = pt_ref[seq, group * PAGE_GROUP + j]
            copies.append(pltpu.make_async_copy(ck_ref.at[layer, page], kbuf.at[slot, j], sem.at[0, slot]))
            copies.append(pltpu.make_async_copy(cv_ref.at[layer, page], vbuf.at[slot, j], sem.at[1, slot]))
        return copies

    @pl.when(i == 0)
    def _():
        for slot in range(N_SLOTS):
            for c in page_copies(0, slot, slot):
                c.start()

    x1, h2 = _residual_and_mlp_input(x_ref, yp_ref, yr_ref, o_ref, mod_ref, wout_ref, gmlp_ref)
    xo_ref[...] = x1
    h2_s[...] = h2
    mlp_s[...] = jnp.zeros(mlp_s.shape, F32)
    qbd_s[...] = _sample_query_rows(q_ref[...])

    def page_group(slot, state):
        kt = jnp.concatenate([kbuf[slot, j].reshape(QK_WIDTH, PAGE_SIZE).astype(BF16)
                              for j in range(PAGE_GROUP)], axis=1)
        s = jnp.dot(qbd_s[...], kt, preferred_element_type=F32)
        v_heads = [jnp.concatenate([vbuf[slot, j, pl.ds(h, PAGE_SIZE, stride=N_HEADS), :].astype(BF16)
                                    for j in range(PAGE_GROUP)], axis=0) for h in range(N_HEADS)]
        return _sample_softmax_update(state, s, v_heads)

    state = (jnp.full((N_SROWS, 128), NEG_BIG, F32), jnp.zeros((N_SROWS, 128), F32),
             jnp.zeros((N_SROWS, V_DIM), F32))
    for group in range(N_GROUPS):
        slot = group % N_SLOTS
        for c in page_copies(i, group, slot):
            c.wait()
        if group % GROUPS_PER_FF_CHUNK == 0:
            mlp_s[...] += _mlp_chunk(h2_s[...], wup_ref, wdown_ref, group // GROUPS_PER_FF_CHUNK)
        state = page_group(slot, state)
        if group + N_SLOTS < N_GROUPS:
            for c in page_copies(i, group + N_SLOTS, slot):
                c.start()
        else:
            @pl.when(i + 1 < n_steps)
            def _():
                for c in page_copies(i + 1, group + N_SLOTS - N_GROUPS, slot):
                    c.start()

    _sample_new_rows_and_finish(state, qbd_s[...], kn_ref, vn_ref, knp_s, vnp_s, lamp_ref, subln_ref, os_ref, lam_init)
    xo_ref[...] = xo_ref[...] + mod_ref[:, 5 * D_MODEL:6 * D_MODEL] * mlp_s[...]


def _out_attn_call(page_table, x, yp, yr, o, mod3, wout_bf, gmlp, wup_bf, wdown_bf,
                   q_bm, k_bm, v_bm, lamp, subln, cache_kt, cache_vr, *, tm, tiles_per_seq, layer, lam_init):
    m = x.shape[0]
    assert m // tm == DEC_BATCH, "one prompt row tile per sample sequence"
    row = lambda w: pl.BlockSpec((tm, w), lambda i, pt: (i, 0))
    full = lambda a: pl.BlockSpec(a.shape, lambda i, pt: (0,) * a.ndim, pipeline_mode=pl.Buffered(1))
    tok = pl.BlockSpec((None, DEC_SEQ, QK_WIDTH), lambda i, pt: (i, 0, 0))
    hbm = pl.BlockSpec(memory_space=pl.ANY)
    grid_spec = pltpu.PrefetchScalarGridSpec(
        num_scalar_prefetch=1,
        grid=(m // tm,),
        in_specs=[row(D_MODEL), row(256), row(256), row(512),
                  pl.BlockSpec((None, 1, 6 * D_MODEL), lambda i, pt: (i // tiles_per_seq, 0, 0)),
                  full(wout_bf), full(gmlp), full(wup_bf), full(wdown_bf),
                  tok, tok, tok, full(lamp), full(subln), hbm, hbm],
        out_specs=[row(D_MODEL), tok],
        scratch_shapes=[pltpu.VMEM((N_SLOTS, PAGE_GROUP, N_HEADS, 2, QK_DIM, PAGE_SIZE), F32),
                        pltpu.VMEM((N_SLOTS, PAGE_GROUP, PAGE_SIZE * N_HEADS, V_DIM), F32),
                        pltpu.SemaphoreType.DMA((2, N_SLOTS)),
                        pltpu.VMEM((tm, D_MODEL), BF16),
                        pltpu.VMEM((tm, D_MODEL), F32),
                        pltpu.VMEM((N_SROWS, QK_WIDTH), BF16),
                        pltpu.VMEM((PAGE_SIZE, QK_WIDTH), F32),
                        pltpu.VMEM((PAGE_SIZE, ATTN_WIDTH), F32)])
    return pl.pallas_call(
        functools.partial(_out_attn_kernel, layer=layer, lam_init=lam_init),
        grid_spec=grid_spec,
        out_shape=[jax.ShapeDtypeStruct((m, D_MODEL), F32),
                   jax.ShapeDtypeStruct((DEC_BATCH, DEC_SEQ, ATTN_WIDTH), F32)],
        compiler_params=_params(1),
        name="out_attn",
    )(page_table, x, yp, yr, o, mod3, wout_bf, gmlp, wup_bf, wdown_bf, q_bm, k_bm, v_bm, lamp, subln,
      cache_kt, cache_vr)


def _block_diag(w):
    g, c, d = w.shape
    eye = jnp.eye(g, dtype=w.dtype)
    return (eye[:, None, :, None] * w[:, :, None, :]).reshape(g * c, g * d)


def _rope_tables(pos):
    half = ROPE_DIM // 2
    freqs = ROPE_THETA ** (-np.arange(half, dtype=np.float64) / half)
    ang = np.asarray(pos, np.float64)[:, None] * freqs[None, :]
    cos, sin = np.cos(ang), np.sin(ang)
    n = ang.shape[0]
    c64 = np.concatenate([cos, cos, np.ones((n, QK_DIM - ROPE_DIM))], axis=1)
    s64 = np.concatenate([-sin, sin, np.zeros((n, QK_DIM - ROPE_DIM))], axis=1)
    return (jnp.asarray(np.concatenate([c64, c64], axis=1), F32),
            jnp.asarray(np.concatenate([s64, s64], axis=1), F32))


def kernel(x_prompt, x_sample, cache_k, cache_v, page_table, state_pool, state_conv, state_rglru,
           c_prompt, c_sample, w_ada, b_ada, g_mix, w_in, pool_w, pool_scale, conv_w, conv_b,
           rg_wa, rg_ba, rg_wx, rg_bx, rg_lambda, q_norm, k_norm, lambda_q1, lambda_k1,
           lambda_q2, lambda_k2, subln, w_out, g_mlp, w_up, w_down):
    n_pool = cache_k.shape[1]
    mp = BATCH * SEQ
    ms = DEC_BATCH * DEC_SEQ
    tm_tiles = SEQ // PROJ_TM

    cos_p, sin_p = _rope_tables(np.arange(SEQ))
    cos_s, sin_s = _rope_tables(PAST_LEN + np.repeat(np.arange(DEC_SEQ), DEC_BATCH))
    grp = _block_diag(jnp.ones((QK_WIDTH // QK_DIM, QK_DIM, QK_DIM), BF16))
    cache_kt = cache_k.transpose(0, 1, 3, 4, 5, 2)
    cache_vr = cache_v.reshape(DEPTH, n_pool, PAGE_SIZE * N_HEADS, V_DIM)
    pad_rows = (-(BATCH + DEC_BATCH)) % 16
    c_all = jnp.concatenate([c_prompt, c_sample, jnp.zeros((pad_rows, D_MODEL), F32)], axis=0)

    def to_tm(a):
        return a.transpose(1, 0, 2).reshape(ms, a.shape[-1])

    def to_bm(a):
        return a.reshape(DEC_SEQ, DEC_BATCH, a.shape[-1]).transpose(1, 0, 2)

    xp_rows = x_prompt.reshape(mp, D_MODEL)
    xs_rows = to_tm(x_sample)
    outs = [[] for _ in range(10)]
    kv_all = None
    for l in range(DEPTH):
        lam_init = 0.8 - 0.6 * math.exp(-0.3 * l)
        mod = _ada_call(c_all, w_ada, b_ada, layer=l)
        mod_p = mod[0:BATCH].reshape(BATCH, 1, 6 * D_MODEL)
        mod_s = jnp.tile(mod[BATCH:BATCH + DEC_BATCH], (DEC_SEQ, 1)).reshape(1, ms, 6 * D_MODEL)
        gmix = g_mix[l].reshape(1, D_MODEL)
        gmlp = g_mlp[l].reshape(1, D_MODEL)
        w_in_bf = w_in[l].astype(BF16)
        wout_bf = w_out[l].astype(BF16)
        wup_bf = w_up[l].astype(BF16)
        wdown_bf = w_down[l].astype(BF16)
        qg = jnp.tile(q_norm[l], QK_WIDTH // QK_DIM).reshape(1, QK_WIDTH)
        kg = jnp.tile(k_norm[l], QK_WIDTH // QK_DIM).reshape(1, QK_WIDTH)
        mix_wts = (_block_diag(pool_w[l]).astype(BF16), pool_scale[l].reshape(1, -1),
                   conv_w[l], conv_b[l].reshape(1, -1),
                   jnp.concatenate([_block_diag(rg_wa[l]), _block_diag(rg_wx[l])], axis=1).astype(BF16),
                   jnp.concatenate([rg_ba[l], rg_bx[l]]).reshape(1, -1),
                   rg_lambda[l].reshape(1, -1))
        lamp = jnp.stack([lambda_q1[l], lambda_k1[l], lambda_q2[l], lambda_k2[l]])
        sub_g = subln[l].reshape(1, V_DIM)

        q, kt_all, v_all, ypool, yrec, hlast, pstate, cstate = _proj_mix_call(
            xp_rows, mod_p, gmix, w_in_bf, cos_p, sin_p, qg, kg, grp, mix_wts, kv_all,
            tm=PROJ_TM, tiles_per_seq=tm_tiles, layer=l, n_layers=DEPTH)
        kv_all = (kt_all, v_all)
        o = _attn_prompt_call(q.reshape(BATCH, SEQ, QK_WIDTH), kt_all,
                              v_all.reshape(DEPTH, BATCH, SEQ * N_HEADS, V_DIM), lamp, sub_g,
                              layer=l, lam_init=lam_init)
        outs[4].append(pstate[:, 16 - POOL_HIST:])
        outs[6].append(cstate[:, 8 - (CONV_WIDTH - 1):])
        outs[8].append(hlast.reshape(BATCH, REC_WIDTH))

        xp_s, xr_s, gate_s, q_s, kt_s, v_s = _proj_call(xs_rows, mod_s, gmix, w_in_bf, cos_s, sin_s, qg, kg, grp,
                                                         tm=ms, tiles_per_seq=1)
        k_s = kt_s[0, 0].T
        hist = (state_pool[l].transpose(1, 0, 2).reshape(POOL_HIST * DEC_BATCH, POOL_WIDTH),
                state_conv[l].transpose(1, 0, 2).reshape((CONV_WIDTH - 1) * DEC_BATCH, REC_WIDTH),
                state_rglru[l])
        ypool_s, yrec_s, hlast_s = _mix_call(xp_s, xr_s, gate_s, hist, mix_wts, n_seq=1, n_t=1,
                                             stride=DEC_BATCH, rows=ms, pos0=PAST_LEN)
        k_bm, v_bm = to_bm(k_s), to_bm(v_s.reshape(ms, ATTN_WIDTH))

        xp_rows, o_s = _out_attn_call(page_table, xp_rows, ypool, yrec, o.reshape(mp, ATTN_WIDTH), mod_p, wout_bf,
                                      gmlp, wup_bf, wdown_bf, to_bm(q_s.astype(F32)), k_bm, v_bm, lamp, sub_g,
                                      cache_kt, cache_vr, tm=OUT_TM, tiles_per_seq=SEQ // OUT_TM, layer=l,
                                      lam_init=lam_init)
        xs_rows = _out_call(xs_rows, ypool_s, yrec_s, to_tm(o_s), mod_s, wout_bf, gmlp,
                            wup_bf, wdown_bf, tm=ms, tiles_per_seq=1)
        outs[2].append(k_bm.reshape(DEC_BATCH, DEC_SEQ, N_HEADS, 2, QK_DIM))
        outs[3].append(v_bm.reshape(DEC_BATCH, DEC_SEQ, N_HEADS, V_DIM))
        outs[5].append(jnp.concatenate([state_pool[l][:, DEC_SEQ:], to_bm(xp_s)], axis=1))
        outs[7].append(to_bm(xr_s)[:, DEC_SEQ - (CONV_WIDTH - 1):])
        outs[9].append(hlast_s.reshape(DEC_BATCH, REC_WIDTH))

    kt_all, v_all = kv_all
    new_k_prompt = kt_all.reshape(DEPTH, BATCH, N_HEADS, 2, QK_DIM, SEQ).transpose(0, 1, 5, 2, 3, 4)
    new_v_prompt = v_all.reshape(DEPTH, BATCH, SEQ, N_HEADS, V_DIM)
    stacked = [jnp.stack(o) for o in outs[2:]]
    return (xp_rows.reshape(BATCH, SEQ, D_MODEL), to_bm(xs_rows), new_k_prompt, new_v_prompt) + tuple(stacked)
```

```python
import functools
import math

import jax
import jax.numpy as jnp
import numpy as np
from jax import lax
from jax.experimental import pallas as pl
from jax.experimental.pallas import tpu as pltpu

D_MODEL = 1024
BATCH = 4
SEQ = 4096
DEPTH = 2
DEC_BATCH = 32
DEC_SEQ = 4
PAST_LEN = 16384
PAGE_SIZE = 128
N_PAGES = PAST_LEN // PAGE_SIZE

POOL_WIDTH = 256
POOL_WINDOWS = (2, 4, 8, 16)
POOL_GROUP = 64
POOL_HIST = 15
REC_WIDTH = 256
CONV_WIDTH = 4
RGLRU_C = 8.0
N_HEADS = 4
QK_DIM = 64
V_DIM = 128
ATTN_WIDTH = N_HEADS * V_DIM
ROPE_DIM = 16
ROPE_THETA = 500000.0
ATTN_SCALE = 1.0 / math.sqrt(QK_DIM)
Q_PRESCALE = ATTN_SCALE * math.log2(math.e)
QK_WIDTH = N_HEADS * 2 * QK_DIM
IN_WIDTH = POOL_WIDTH + 2 * REC_WIDTH + 2 * QK_WIDTH + ATTN_WIDTH
D_FF = 4 * D_MODEL
EPS = 1e-6

F32 = jnp.float32
BF16 = jnp.bfloat16
NEG_BIG = -1e30
V7X_VMEM_LIMIT = 56 * 1024 * 1024

PROJ_TM = 512
OUT_TM = BATCH * SEQ // DEC_BATCH
ATTN_Q_CHUNK = 128
PAGE_GROUP = 8
N_GROUPS = N_PAGES // PAGE_GROUP
N_SLOTS = 4
GROUPS_PER_FF_CHUNK = 2
N_FF_CHUNKS = N_GROUPS // GROUPS_PER_FF_CHUNK
FF_CHUNK = D_FF // N_FF_CHUNKS


def _bdot(a, b):
    return jnp.dot(a.astype(BF16), b.astype(BF16), preferred_element_type=F32)


def _params(n_axes):
    return pltpu.CompilerParams(dimension_semantics=("arbitrary",) * n_axes,
                                vmem_limit_bytes=V7X_VMEM_LIMIT)


def _ada_kernel(c_ref, w_ref, b_ref, o_ref):
    c = c_ref[...]
    o_ref[...] = _bdot(c * jax.nn.sigmoid(c), w_ref[...]) + b_ref[...]


def _ada_call(c_all, w_ada, b_ada, *, layer):
    rows = c_all.shape[0]
    tn = 1536
    return pl.pallas_call(
        _ada_kernel,
        grid=(6 * D_MODEL // tn,),
        in_specs=[pl.BlockSpec((rows, D_MODEL), lambda j: (0, 0)),
                  pl.BlockSpec((None, D_MODEL, tn), lambda j: (layer, 0, j)),
                  pl.BlockSpec((None, 1, tn), lambda j: (layer, 0, j))],
        out_specs=pl.BlockSpec((rows, tn), lambda j: (0, j)),
        out_shape=jax.ShapeDtypeStruct((rows, 6 * D_MODEL), F32),
        compiler_params=_params(1),
        name="ada",
    )(c_all, w_ada, b_ada.reshape(DEPTH, 1, 6 * D_MODEL))


def _proj_body(x_ref, mod_ref, gmix_ref, w_ref, cos_ref, sin_ref, qg_ref, kg_ref, grp_ref,
               xp_ref, xr_ref, gate_ref, q_ref, kt_ref, v_ref):
    x = x_ref[...]
    y = x * lax.rsqrt(jnp.mean(x * x, axis=-1, keepdims=True) + EPS) * gmix_ref[...]
    sh1 = mod_ref[:, 0:D_MODEL]
    sc1 = mod_ref[:, D_MODEL:2 * D_MODEL]
    h = y * (1.0 + sc1) + sh1
    proj = _bdot(h, w_ref[...])
    xp_ref[...] = proj[:, 0:256]
    xr_ref[...] = proj[:, 256:512]
    gate_ref[...] = proj[:, 512:768]
    tm = x.shape[0]
    for slot in range(v_ref.shape[0]):
        for hd in range(N_HEADS):
            v_ref[slot, pl.ds(hd, tm, stride=N_HEADS), :] = proj[:, 1792 + hd * V_DIM:1792 + (hd + 1) * V_DIM]

    cos = jnp.concatenate([cos_ref[...]] * N_HEADS, axis=1)
    sin = jnp.concatenate([sin_ref[...]] * N_HEADS, axis=1)
    lane = lax.broadcasted_iota(jnp.int32, cos.shape, 1) % QK_DIM
    grp = grp_ref[...]

    def norm_rope(z, g):
        ssq = _bdot(z * z, grp)
        zn = z * lax.rsqrt(ssq * (1.0 / QK_DIM) + EPS) * g
        half = ROPE_DIM // 2
        swapped = jnp.where(lane < half, pltpu.roll(zn, QK_WIDTH - half, 1), pltpu.roll(zn, half, 1))
        return zn * cos + swapped * sin

    q = norm_rope(proj[:, 768:1280], qg_ref[...])
    q_ref[...] = (q * Q_PRESCALE).astype(q_ref.dtype)
    kt = norm_rope(proj[:, 1280:1792], kg_ref[...]).T
    for slot in range(kt_ref.shape[0]):
        kt_ref[slot] = kt


def _proj_kernel(*refs):
    _proj_body(*refs)


def _proj_io(x, mod3, gmix, w_in_bf, cos_t, sin_t, qg, kg, grp, *, tm, tiles_per_seq):
    mod_rows = mod3.shape[1]
    full = lambda a: pl.BlockSpec(a.shape, lambda i: (0,) * a.ndim)
    ins = [x, mod3, gmix, w_in_bf, cos_t, sin_t, qg, kg, grp]
    in_specs = [pl.BlockSpec((tm, D_MODEL), lambda i: (i, 0)),
                pl.BlockSpec((None, mod_rows, 6 * D_MODEL), lambda i: (i // tiles_per_seq, 0, 0)),
                full(gmix), full(w_in_bf),
                pl.BlockSpec((tm, V_DIM), lambda i: (i % tiles_per_seq, 0)),
                pl.BlockSpec((tm, V_DIM), lambda i: (i % tiles_per_seq, 0)),
                full(qg), full(kg), full(grp)]
    return ins, in_specs


def _kv_out(m, tm, tiles_per_seq, layer, n_layers):
    n_seq = m // (tm * tiles_per_seq)
    seq_len = tm * tiles_per_seq
    slots, first = (n_layers, 0) if layer == 0 else (1, layer)
    specs = [pl.BlockSpec((slots, None, QK_WIDTH, tm), lambda i: (first, i // tiles_per_seq, 0, i % tiles_per_seq)),
             pl.BlockSpec((slots, tm * N_HEADS, V_DIM), lambda i: (first, i, 0))]
    shapes = [jax.ShapeDtypeStruct((n_layers, n_seq, QK_WIDTH, seq_len), F32),
              jax.ShapeDtypeStruct((n_layers, m * N_HEADS, V_DIM), F32)]
    return specs, shapes


def _proj_call(x, mod3, gmix, w_in_bf, cos_t, sin_t, qg, kg, grp, *, tm, tiles_per_seq):
    m = x.shape[0]
    ins, in_specs = _proj_io(x, mod3, gmix, w_in_bf, cos_t, sin_t, qg, kg, grp, tm=tm, tiles_per_seq=tiles_per_seq)
    kv_specs, kv_shapes = _kv_out(m, tm, tiles_per_seq, 0, 1)
    row = lambda w: pl.BlockSpec((tm, w), lambda i: (i, 0))
    return pl.pallas_call(
        _proj_kernel,
        grid=(m // tm,),
        in_specs=in_specs,
        out_specs=[row(256), row(256), row(256), row(512)] + kv_specs,
        out_shape=[jax.ShapeDtypeStruct((m, 256), F32)] * 3 + [jax.ShapeDtypeStruct((m, 512), BF16)] + kv_shapes,
        compiler_params=_params(1),
        name="proj",
    )(*ins)


def _mix_kernel(*refs, stride, rows, pos0, has_hist):
    hist = refs[3:6] if has_hist else None
    rest = refs[6:] if has_hist else refs[3:]
    t = pl.program_id(1)
    _mix_prepare(t, hist, *rest[-3:], stride=stride, rows=rows)
    _mix_body(t, refs[0], refs[1], refs[2], *rest, stride=stride, rows=rows, pos0=pos0)


def _mix_prepare(t, hist, pext, cext, hcar, *, stride, rows):
    has_hist = hist is not None
    if has_hist:
        hp_ref, hc_ref, h0_ref = hist
    hp_rows = 16 * stride
    hc_rows = 8 * stride

    @pl.when(t == 0)
    def _():
        pext[0:hp_rows, :] = jnp.zeros((hp_rows, POOL_WIDTH), F32)
        cext[0:hc_rows, :] = jnp.zeros((hc_rows, REC_WIDTH), F32)
        if has_hist:
            pext[stride:hp_rows, :] = hp_ref[...]
            cext[(8 - (CONV_WIDTH - 1)) * stride:hc_rows, :] = hc_ref[...]
            hcar[...] = h0_ref[...]
        else:
            hcar[...] = jnp.zeros((stride, REC_WIDTH), F32)

    @pl.when(t > 0)
    def _():
        pext[0:hp_rows, :] = pext[rows:rows + hp_rows, :]
        cext[0:hc_rows, :] = cext[rows:rows + hc_rows, :]


def _mix_body(t, xp_ref, xr_ref, gate_ref, poolw_ref, pscale_ref, convw_ref, convb_ref,
              wax_ref, bax_ref, lam_ref, ypool_ref, yrec_ref, hlast_ref, pext, cext, hcar, *, stride, rows, pos0,
              scan_scratch=None):
    tt = rows // stride
    hp_rows = 16 * stride
    hc_rows = 8 * stride
    pext[hp_rows:hp_rows + rows, :] = xp_ref[...]
    cext[hc_rows:hc_rows + rows, :] = xr_ref[...]

    tidx = lax.broadcasted_iota(jnp.int32, (rows, 128), 0) // stride
    lane = lax.broadcasted_iota(jnp.int32, (rows, 128), 1)
    pos1 = (pos0 + 1 + t * tt + tidx).astype(F32)

    ext = pext[...]
    s2 = ext + pltpu.roll(ext, stride, 0)
    s4 = s2 + pltpu.roll(s2, 2 * stride, 0)
    s8 = s4 + pltpu.roll(s4, 4 * stride, 0)
    s16 = s8 + pltpu.roll(s8, 8 * stride, 0)
    new = slice(hp_rows, hp_rows + rows)

    def window_means(lo, small, big, w_small):
        cnt = jnp.where(lane < POOL_GROUP, jnp.minimum(float(w_small), pos1),
                        jnp.minimum(float(2 * w_small), pos1))
        return jnp.where(lane < POOL_GROUP, small[new, lo:lo + 128], big[new, lo:lo + 128]) / cnt

    d = jnp.concatenate([window_means(0, s2, s4, 2), window_means(128, s8, s16, 8)], axis=1) - ext[new, :]
    ypool_ref[...] = (_bdot(d, poolw_ref[...]) * pscale_ref[...]).astype(ypool_ref.dtype)

    y = convb_ref[...]
    for tap in range(CONV_WIDTH):
        off = hc_rows - (CONV_WIDTH - 1 - tap) * stride
        y = y + cext[off:off + rows, :] * convw_ref[tap:tap + 1, :]
    rg = _bdot(y, wax_ref[...]) + bax_ref[...]
    r = jax.nn.sigmoid(rg[:, 0:REC_WIDTH])
    i = jax.nn.sigmoid(rg[:, REC_WIDTH:2 * REC_WIDTH])
    nl = -lam_ref[...]
    softplus = jnp.maximum(nl, 0.0) + jnp.log1p(jnp.exp(-jnp.abs(nl)))
    a = jnp.exp(-RGLRU_C * r * softplus)
    b = jnp.sqrt(1.0 - a * a) * (i * y)

    def doubling_scan(a, b, idx, n, shift):
        step = 1
        while step < n:
            a_prev = pltpu.roll(a, step * shift, 0)
            b_prev = pltpu.roll(b, step * shift, 0)
            valid = idx >= step
            b = jnp.where(valid, a * b_prev + b, b)
            a = jnp.where(valid, a * a_prev, a)
            step *= 2
        return a, b

    hc = hcar[...]
    row = lax.broadcasted_iota(jnp.int32, (rows, REC_WIDTH), 0)
    if scan_scratch is None:
        a, b = doubling_scan(a, b, row // stride, tt, stride)
        carry = jnp.broadcast_to(hc, (rows, REC_WIDTH)) if stride == 1 else jnp.concatenate([hc] * tt, axis=0)
    else:
        a_s, b_s, hin_s = scan_scratch
        n_groups = rows // 8
        a, b = doubling_scan(a, b, row % 8, 8, 1)
        halves = range(REC_WIDTH // 128)

        def put(ref, val):
            for k in halves:
                ref[k] = val[:, 128 * k:128 * (k + 1)]

        put(a_s, a)
        put(b_s, b)
        ga = jnp.concatenate([a_s[k, pl.ds(7, n_groups, stride=8), :] for k in halves], axis=1)
        gb = jnp.concatenate([b_s[k, pl.ds(7, n_groups, stride=8), :] for k in halves], axis=1)
        gidx = lax.broadcasted_iota(jnp.int32, (n_groups, REC_WIDTH), 0)
        ga, gb = doubling_scan(ga, gb, gidx, n_groups, 1)
        after = gb + ga * hc
        put(hin_s, jnp.where(gidx == 0, jnp.broadcast_to(hc, after.shape), pltpu.roll(after, 1, 0)))
        carry = jnp.concatenate(
            [jnp.concatenate([hin_s[k, pl.ds(g, 8, stride=0), :] for g in range(n_groups)], axis=0) for k in halves],
            axis=1)
    h = b + a * carry
    h_end = h[rows - stride:rows, :]
    hcar[...] = h_end
    hlast_ref[...] = h_end

    g = gate_ref[...]
    gelu = 0.5 * g * (1.0 + jnp.tanh(math.sqrt(2.0 / math.pi) * (g + 0.044715 * (g * g * g))))
    yrec_ref[...] = (h * gelu).astype(yrec_ref.dtype)


def _mix_call(xp, xr, gate, hist, wts, *, n_seq, n_t, stride, rows, pos0):
    m = xp.shape[0]
    has_hist = hist is not None
    row = pl.BlockSpec((rows, 256), lambda b, t: (b * n_t + t, 0))
    full = lambda a: pl.BlockSpec(a.shape, lambda b, t: (0,) * a.ndim)
    ins = [xp, xr, gate] + (list(hist) if has_hist else []) + list(wts)
    in_specs = [row, row, row] + [full(a) for a in ins[3:]]
    return pl.pallas_call(
        functools.partial(_mix_kernel, stride=stride, rows=rows, pos0=pos0, has_hist=has_hist),
        grid=(n_seq, n_t),
        in_specs=in_specs,
        out_specs=[row, row, pl.BlockSpec((None, stride, REC_WIDTH), lambda b, t: (b, 0, 0))],
        out_shape=[jax.ShapeDtypeStruct((m, 256), BF16), jax.ShapeDtypeStruct((m, 256), BF16),
                   jax.ShapeDtypeStruct((n_seq, stride, REC_WIDTH), F32)],
        scratch_shapes=[pltpu.VMEM((16 * stride + rows, POOL_WIDTH), F32),
                        pltpu.VMEM((8 * stride + rows, REC_WIDTH), F32),
                        pltpu.VMEM((stride, REC_WIDTH), F32)],
        compiler_params=_params(2),
        name="mix",
    )(*ins)


N_MIX_WTS = 7


def _proj_mix_kernel(*refs, rows, tiles_per_seq, aliased):
    proj_in = refs[0:9]
    mix_wts = refs[9:9 + N_MIX_WTS]
    outs = refs[9 + N_MIX_WTS + (2 if aliased else 0):]
    q_ref, kt_ref, v_ref, ypool_ref, yrec_ref, hlast_ref, pstate_ref, cstate_ref = outs[0:8]
    xp_s, xr_s, gate_s, pext, cext, hcar, scan_a, scan_b, scan_h = outs[8:]
    t = pl.program_id(0) % tiles_per_seq
    _mix_prepare(t, None, pext, cext, hcar, stride=1, rows=rows)
    _proj_body(*proj_in, xp_s, xr_s, gate_s, q_ref, kt_ref, v_ref)
    _mix_body(t, xp_s, xr_s, gate_s, *mix_wts, ypool_ref, yrec_ref, hlast_ref, pext, cext, hcar,
              stride=1, rows=rows, pos0=0, scan_scratch=(scan_a, scan_b, scan_h))
    pstate_ref[...] = pext[rows:rows + 16, :]
    cstate_ref[...] = cext[rows:rows + 8, :]


def _proj_mix_call(x, mod3, gmix, w_in_bf, cos_t, sin_t, qg, kg, grp, mix_wts, kv_prev, *,
                   tm, tiles_per_seq, layer, n_layers):
    m = x.shape[0]
    n_seq = m // (tm * tiles_per_seq)
    ins, in_specs = _proj_io(x, mod3, gmix, w_in_bf, cos_t, sin_t, qg, kg, grp, tm=tm, tiles_per_seq=tiles_per_seq)
    full = lambda a: pl.BlockSpec(a.shape, lambda i: (0,) * a.ndim)
    ins += list(mix_wts)
    in_specs += [full(a) for a in mix_wts]
    aliases = {}
    if kv_prev is not None:
        aliases = {len(ins): 1, len(ins) + 1: 2}
        ins += list(kv_prev)
        in_specs += [pl.BlockSpec(memory_space=pl.ANY)] * 2
    kv_specs, kv_shapes = _kv_out(m, tm, tiles_per_seq, layer, n_layers)
    row = lambda w: pl.BlockSpec((tm, w), lambda i: (i, 0))
    per_seq = lambda r: pl.BlockSpec((None, r, 256), lambda i: (i // tiles_per_seq, 0, 0))
    return pl.pallas_call(
        functools.partial(_proj_mix_kernel, rows=tm, tiles_per_seq=tiles_per_seq, aliased=kv_prev is not None),
        grid=(m // tm,),
        in_specs=in_specs,
        out_specs=[row(512)] + kv_specs + [row(256), row(256), per_seq(1), per_seq(16), per_seq(8)],
        out_shape=[jax.ShapeDtypeStruct((m, 512), BF16)] + kv_shapes
                  + [jax.ShapeDtypeStruct((m, 256), BF16), jax.ShapeDtypeStruct((m, 256), BF16),
                     jax.ShapeDtypeStruct((n_seq, 1, REC_WIDTH), F32),
                     jax.ShapeDtypeStruct((n_seq, 16, POOL_WIDTH), F32),
                     jax.ShapeDtypeStruct((n_seq, 8, REC_WIDTH), F32)],
        scratch_shapes=[pltpu.VMEM((tm, POOL_WIDTH), F32), pltpu.VMEM((tm, REC_WIDTH), F32),
                        pltpu.VMEM((tm, REC_WIDTH), F32),
                        pltpu.VMEM((16 + tm, POOL_WIDTH), F32), pltpu.VMEM((8 + tm, REC_WIDTH), F32),
                        pltpu.VMEM((1, REC_WIDTH), F32),
                        pltpu.VMEM((REC_WIDTH // 128, tm, 128), F32), pltpu.VMEM((REC_WIDTH // 128, tm, 128), F32),
                        pltpu.VMEM((REC_WIDTH // 128, tm // 8, 128), F32)],
        input_output_aliases=aliases,
        compiler_params=_params(1),
        name="proj_mix",
    )(*ins)


def _diff_lambda(lamp_ref, lam_init):
    lp = lamp_ref[...]
    s1 = jnp.sum(lp[0:1, :] * lp[1:2, :], axis=-1, keepdims=True)
    s2 = jnp.sum(lp[2:3, :] * lp[3:4, :], axis=-1, keepdims=True)
    return jnp.exp(s1) - jnp.exp(s2) + lam_init


def _subln(o, g, lam_init):
    return o * lax.rsqrt(jnp.mean(o * o, axis=-1, keepdims=True) + EPS) * g * (1.0 - lam_init)


def _attn_prompt_kernel(q_ref, k_ref, v_ref, lamp_ref, subln_ref, o_ref, kb, vb, *, chunk, lam_init):
    kb[...] = k_ref[...].astype(BF16)
    head = pl.program_id(1)
    vb[:, 0:V_DIM] = v_ref[pl.ds(head, SEQ, stride=N_HEADS), :].astype(BF16)
    vb[:, V_DIM:2 * V_DIM] = jnp.ones((SEQ, V_DIM), BF16)
    lam = _diff_lambda(lamp_ref, lam_init)
    gain = subln_ref[...]
    lane = lax.broadcasted_iota(jnp.int32, (chunk, 2 * QK_DIM), 1)
    r = lax.broadcasted_iota(jnp.int32, (2 * chunk, chunk), 0) % chunk
    c = lax.broadcasted_iota(jnp.int32, (2 * chunk, chunk), 1)
    causal = c <= r
    for r0 in range(0, SEQ, chunk):
        nk = r0 + chunk
        q = q_ref[r0:r0 + chunk, :]
        zero = jnp.zeros_like(q)
        q2 = jnp.concatenate([jnp.where(lane < QK_DIM, q, zero), jnp.where(lane >= QK_DIM, q, zero)], axis=0)
        s = jnp.dot(q2, kb[:, 0:nk], preferred_element_type=F32)
        tail = jnp.where(causal, s[:, r0:nk], NEG_BIG)
        s = tail if r0 == 0 else jnp.concatenate([s[:, 0:r0], tail], axis=1)
        m = jnp.max(s, axis=-1, keepdims=True)
        e = jnp.exp2(s - m).astype(BF16)
        pv = jnp.dot(e, vb[0:nk, :], preferred_element_type=F32)
        a0 = pv[0:chunk, :]
        a1 = pv[chunk:2 * chunk, :]
        o = a0[:, 0:V_DIM] / a0[:, V_DIM:2 * V_DIM] - lam * (a1[:, 0:V_DIM] / a1[:, V_DIM:2 * V_DIM])
        o_ref[r0:r0 + chunk, :] = _subln(o, gain, lam_init).astype(o_ref.dtype)


def _attn_prompt_call(q, kt_all, v_all, lamp, subln, *, layer, lam_init):
    qspec = pl.BlockSpec((None, SEQ, V_DIM), lambda b, h: (b, 0, h))
    full = lambda a: pl.BlockSpec(a.shape, lambda b, h: (0,) * a.ndim)
    return pl.pallas_call(
        functools.partial(_attn_prompt_kernel, chunk=ATTN_Q_CHUNK, lam_init=lam_init),
        grid=(BATCH, N_HEADS),
        in_specs=[qspec,
                  pl.BlockSpec((None, None, 2 * QK_DIM, SEQ), lambda b, h: (layer, b, h, 0)),
                  pl.BlockSpec((None, None, SEQ * N_HEADS, V_DIM), lambda b, h: (layer, b, 0, 0)),
                  full(lamp), full(subln)],
        out_specs=qspec,
        out_shape=jax.ShapeDtypeStruct((BATCH, SEQ, ATTN_WIDTH), BF16),
        scratch_shapes=[pltpu.VMEM((2 * QK_DIM, SEQ), BF16),
                        pltpu.VMEM((SEQ, 2 * V_DIM), BF16)],
        compiler_params=_params(2),
        name="attn_prompt",
    )(q, kt_all, v_all, lamp, subln)


N_SROWS = DEC_SEQ * N_HEADS * 2


def _sample_query_rows(q):
    r = lax.broadcasted_iota(jnp.int32, (N_SROWS, QK_WIDTH), 0)
    g = lax.broadcasted_iota(jnp.int32, (N_SROWS, QK_WIDTH), 1) // QK_DIM
    rows = jnp.zeros((N_SROWS, QK_WIDTH), F32)
    for t in range(DEC_SEQ):
        rows = jnp.where(r % DEC_SEQ == t, jnp.broadcast_to(q[t:t + 1, :], (N_SROWS, QK_WIDTH)), rows)
    return jnp.where(g == r // DEC_SEQ, rows, 0.0).astype(BF16)


def _sample_softmax_update(state, s, v_heads):
    m_old, l_old, acc_old = state
    m_new = jnp.maximum(m_old, jnp.max(s, axis=-1, keepdims=True))
    alpha = jnp.exp2(m_old - m_new)
    e = jnp.concatenate([jnp.exp2(s[:, j:j + 128] - m_new) for j in range(0, s.shape[1], 128)], axis=1)
    l_new = alpha * l_old + jnp.sum(e, axis=-1, keepdims=True)
    eb = e.astype(BF16)
    pv = [jnp.dot(eb[8 * h:8 * h + 8, :], v_heads[h], preferred_element_type=F32) for h in range(N_HEADS)]
    return m_new, l_new, alpha * acc_old + jnp.concatenate(pv, axis=0)


def _sample_new_rows_and_finish(state, qbd, kn_ref, vn_ref, knp_ref, vnp_ref, lamp_ref, subln_ref, os_ref, lam_init):
    knp_ref[...] = jnp.zeros((PAGE_SIZE, QK_WIDTH), F32)
    vnp_ref[...] = jnp.zeros((PAGE_SIZE, ATTN_WIDTH), F32)
    knp_ref[0:DEC_SEQ, :] = kn_ref[...]
    vnp_ref[0:DEC_SEQ, :] = vn_ref[...]
    knp = knp_ref[...].astype(BF16)
    vnp = vnp_ref[...].astype(BF16)
    s_new = lax.dot_general(qbd, knp, (((1,), (1,)), ((), ())), preferred_element_type=F32)
    row_t = lax.broadcasted_iota(jnp.int32, (N_SROWS, PAGE_SIZE), 0) % DEC_SEQ
    col = lax.broadcasted_iota(jnp.int32, (N_SROWS, PAGE_SIZE), 1)
    state = _sample_softmax_update(state, jnp.where(col <= row_t, s_new, NEG_BIG),
                                   [vnp[:, h * V_DIM:(h + 1) * V_DIM] for h in range(N_HEADS)])
    _, l, acc = state
    on = acc / l
    lam = _diff_lambda(lamp_ref, lam_init)
    gain = subln_ref[...]
    for h in range(N_HEADS):
        blk = on[8 * h:8 * h + 8, :]
        diff = blk - lam * pltpu.roll(blk, DEC_SEQ, 0)
        os_ref[:, h * V_DIM:(h + 1) * V_DIM] = _subln(diff, gain, lam_init)[0:DEC_SEQ, :]


def _residual_and_mlp_input(x_ref, yp_ref, yr_ref, o_ref, mod_ref, wout_ref, gmlp_ref):
    mix = (jnp.dot(yp_ref[...].astype(BF16), wout_ref[0:256, :], preferred_element_type=F32)
           + jnp.dot(yr_ref[...].astype(BF16), wout_ref[256:512, :], preferred_element_type=F32)
           + jnp.dot(o_ref[...].astype(BF16), wout_ref[512:1024, :], preferred_element_type=F32))
    g1 = mod_ref[:, 2 * D_MODEL:3 * D_MODEL]
    sh2 = mod_ref[:, 3 * D_MODEL:4 * D_MODEL]
    sc2 = mod_ref[:, 4 * D_MODEL:5 * D_MODEL]
    x1 = x_ref[...] + g1 * mix
    y = x1 * lax.rsqrt(jnp.mean(x1 * x1, axis=-1, keepdims=True) + EPS) * gmlp_ref[...]
    return x1, (y * (1.0 + sc2) + sh2).astype(BF16)


def _mlp_chunk(h2, wup_ref, wdown_ref, c):
    cols = slice(c * FF_CHUNK, (c + 1) * FF_CHUNK)
    up = jnp.dot(h2, wup_ref[:, cols], preferred_element_type=F32)
    act = jnp.square(jnp.maximum(up, 0.0)).astype(BF16)
    return jnp.dot(act, wdown_ref[cols, :], preferred_element_type=F32)


def _out_kernel(x_ref, yp_ref, yr_ref, o_ref, mod_ref, wout_ref, gmlp_ref, wup_ref, wdown_ref, xo_ref):
    x1, h2 = _residual_and_mlp_input(x_ref, yp_ref, yr_ref, o_ref, mod_ref, wout_ref, gmlp_ref)
    mlp = _mlp_chunk(h2, wup_ref, wdown_ref, 0)
    for c in range(1, N_FF_CHUNKS):
        mlp = mlp + _mlp_chunk(h2, wup_ref, wdown_ref, c)
    xo_ref[...] = x1 + mod_ref[:, 5 * D_MODEL:6 * D_MODEL] * mlp


def _out_call(x, yp, yr, o, mod3, wout_bf, gmlp, wup_bf, wdown_bf, *, tm, tiles_per_seq):
    m = x.shape[0]
    mod_rows = mod3.shape[1]
    row = lambda w: pl.BlockSpec((tm, w), lambda i: (i, 0))
    full = lambda a: pl.BlockSpec(a.shape, lambda i: (0,) * a.ndim, pipeline_mode=pl.Buffered(1))
    return pl.pallas_call(
        _out_kernel,
        grid=(m // tm,),
        in_specs=[row(D_MODEL), row(256), row(256), row(512),
                  pl.BlockSpec((None, mod_rows, 6 * D_MODEL), lambda i: (i // tiles_per_seq, 0, 0)),
                  full(wout_bf), full(gmlp), full(wup_bf), full(wdown_bf)],
        out_specs=row(D_MODEL),
        out_shape=jax.ShapeDtypeStruct((m, D_MODEL), F32),
        compiler_params=_params(1),
        name="out",
    )(x, yp, yr, o, mod3, wout_bf, gmlp, wup_bf, wdown_bf)


def _out_attn_kernel(pt_ref, x_ref, yp_ref, yr_ref, o_ref, mod_ref, wout_ref, gmlp_ref, wup_ref, wdown_ref,
                     q_ref, kn_ref, vn_ref, lamp_ref, subln_ref, ck_ref, cv_ref,
                     xo_ref, os_ref,
                     kbuf, vbuf, sem, h2_s, mlp_s, qbd_s, knp_s, vnp_s, *, layer, lam_init):
    i = pl.program_id(0)
    n_steps = pl.num_programs(0)

    def page_copies(seq, group, slot):
        copies = []
        for j in range(PAGE_GROUP):
            page = pt_ref[seq, group * PAGE_GROUP + j]
            copies.append(pltpu.make_async_copy(ck_ref.at[layer, page], kbuf.at[slot, j], sem.at[0, slot]))
            copies.append(pltpu.make_async_copy(cv_ref.at[layer, page], vbuf.at[slot, j], sem.at[1, slot]))
        return copies

    @pl.when(i == 0)
    def _():
        for slot in range(N_SLOTS):
            for c in page_copies(0, slot, slot):
                c.start()

    x1, h2 = _residual_and_mlp_input(x_ref, yp_ref, yr_ref, o_ref, mod_ref, wout_ref, gmlp_ref)
    xo_ref[...] = x1
    h2_s[...] = h2
    mlp_s[...] = jnp.zeros(mlp_s.shape, F32)
    qbd_s[...] = _sample_query_rows(q_ref[...])

    def page_group(slot, state):
        kt = jnp.concatenate([kbuf[slot, j].reshape(QK_WIDTH, PAGE_SIZE).astype(BF16)
                              for j in range(PAGE_GROUP)], axis=1)
        s = jnp.dot(qbd_s[...], kt, preferred_element_type=F32)
        v_heads = [jnp.concatenate([vbuf[slot, j, pl.ds(h, PAGE_SIZE, stride=N_HEADS), :].astype(BF16)
                                    for j in range(PAGE_GROUP)], axis=0) for h in range(N_HEADS)]
        return _sample_softmax_update(state, s, v_heads)

    state = (jnp.full((N_SROWS, 128), NEG_BIG, F32), jnp.zeros((N_SROWS, 128), F32),
             jnp.zeros((N_SROWS, V_DIM), F32))
    for group in range(N_GROUPS):
        slot = group % N_SLOTS
        for c in page_copies(i, group, slot):
            c.wait()
        if group % GROUPS_PER_FF_CHUNK == 0:
            mlp_s[...] += _mlp_chunk(h2_s[...], wup_ref, wdown_ref, group // GROUPS_PER_FF_CHUNK)
        state = page_group(slot, state)
        if group + N_SLOTS < N_GROUPS:
            for c in page_copies(i, group + N_SLOTS, slot):
                c.start()
        else:
            @pl.when(i + 1 < n_steps)
            def _():
                for c in page_copies(i + 1, group + N_SLOTS - N_GROUPS, slot):
                    c.start()

    _sample_new_rows_and_finish(state, qbd_s[...], kn_ref, vn_ref, knp_s, vnp_s, lamp_ref, subln_ref, os_ref, lam_init)
    xo_ref[...] = xo_ref[...] + mod_ref[:, 5 * D_MODEL:6 * D_MODEL] * mlp_s[...]


def _out_attn_call(page_table, x, yp, yr, o, mod3, wout_bf, gmlp, wup_bf, wdown_bf,
                   q_bm, k_bm, v_bm, lamp, subln, cache_kt, cache_vr, *, tm, tiles_per_seq, layer, lam_init):
    m = x.shape[0]
    assert m // tm == DEC_BATCH, "one prompt row tile per sample sequence"
    row = lambda w: pl.BlockSpec((tm, w), lambda i, pt: (i, 0))
    full = lambda a: pl.BlockSpec(a.shape, lambda i, pt: (0,) * a.ndim, pipeline_mode=pl.Buffered(1))
    tok = pl.BlockSpec((None, DEC_SEQ, QK_WIDTH), lambda i, pt: (i, 0, 0))
    hbm = pl.BlockSpec(memory_space=pl.ANY)
    grid_spec = pltpu.PrefetchScalarGridSpec(
        num_scalar_prefetch=1,
        grid=(m // tm,),
        in_specs=[row(D_MODEL), row(256), row(256), row(512),
                  pl.BlockSpec((None, 1, 6 * D_MODEL), lambda i, pt: (i // tiles_per_seq, 0, 0)),
                  full(wout_bf), full(gmlp), full(wup_bf), full(wdown_bf),
                  tok, tok, tok, full(lamp), full(subln), hbm, hbm],
        out_specs=[row(D_MODEL), tok],
        scratch_shapes=[pltpu.VMEM((N_SLOTS, PAGE_GROUP, N_HEADS, 2, QK_DIM, PAGE_SIZE), F32),
                        pltpu.VMEM((N_SLOTS, PAGE_GROUP, PAGE_SIZE * N_HEADS, V_DIM), F32),
                        pltpu.SemaphoreType.DMA((2, N_SLOTS)),
                        pltpu.VMEM((tm, D_MODEL), BF16),
                        pltpu.VMEM((tm, D_MODEL), F32),
                        pltpu.VMEM((N_SROWS, QK_WIDTH), BF16),
                        pltpu.VMEM((PAGE_SIZE, QK_WIDTH), F32),
                        pltpu.VMEM((PAGE_SIZE, ATTN_WIDTH), F32)])
    return pl.pallas_call(
        functools.partial(_out_attn_kernel, layer=layer, lam_init=lam_init),
        grid_spec=grid_spec,
        out_shape=[jax.ShapeDtypeStruct((m, D_MODEL), F32),
                   jax.ShapeDtypeStruct((DEC_BATCH, DEC_SEQ, ATTN_WIDTH), F32)],
        compiler_params=_params(1),
        name="out_attn",
    )(page_table, x, yp, yr, o, mod3, wout_bf, gmlp, wup_bf, wdown_bf, q_bm, k_bm, v_bm, lamp, subln,
      cache_kt, cache_vr)


def _block_diag(w):
    g, c, d = w.shape
    eye = jnp.eye(g, dtype=w.dtype)
    return (eye[:, None, :, None] * w[:, :, None, :]).reshape(g * c, g * d)


def _rope_tables(pos):
    half = ROPE_DIM // 2
    freqs = ROPE_THETA ** (-np.arange(half, dtype=np.float64) / half)
    ang = np.asarray(pos, np.float64)[:, None] * freqs[None, :]
    cos, sin = np.cos(ang), np.sin(ang)
    n = ang.shape[0]
    c64 = np.concatenate([cos, cos, np.ones((n, QK_DIM - ROPE_DIM))], axis=1)
    s64 = np.concatenate([-sin, sin, np.zeros((n, QK_DIM - ROPE_DIM))], axis=1)
    return (jnp.asarray(np.concatenate([c64, c64], axis=1), F32),
            jnp.asarray(np.concatenate([s64, s64], axis=1), F32))


def kernel(x_prompt, x_sample, cache_k, cache_v, page_table, state_pool, state_conv, state_rglru,
           c_prompt, c_sample, w_ada, b_ada, g_mix, w_in, pool_w, pool_scale, conv_w, conv_b,
           rg_wa, rg_ba, rg_wx, rg_bx, rg_lambda, q_norm, k_norm, lambda_q1, lambda_k1,
           lambda_q2, lambda_k2, subln, w_out, g_mlp, w_up, w_down):
    n_pool = cache_k.shape[1]
    mp = BATCH * SEQ
    ms = DEC_BATCH * DEC_SEQ
    tm_tiles = SEQ // PROJ_TM

    cos_p, sin_p = _rope_tables(np.arange(SEQ))
    cos_s, sin_s = _rope_tables(PAST_LEN + np.repeat(np.arange(DEC_SEQ), DEC_BATCH))
    grp = _block_diag(jnp.ones((QK_WIDTH // QK_DIM, QK_DIM, QK_DIM), BF16))
    cache_kt = cache_k.transpose(0, 1, 3, 4, 5, 2)
    cache_vr = cache_v.reshape(DEPTH, n_pool, PAGE_SIZE * N_HEADS, V_DIM)
    pad_rows = (-(BATCH + DEC_BATCH)) % 16
    c_all = jnp.concatenate([c_prompt, c_sample, jnp.zeros((pad_rows, D_MODEL), F32)], axis=0)

    def to_tm(a):
        return a.transpose(1, 0, 2).reshape(ms, a.shape[-1])

    def to_bm(a):
        return a.reshape(DEC_SEQ, DEC_BATCH, a.shape[-1]).transpose(1, 0, 2)

    xp_rows = x_prompt.reshape(mp, D_MODEL)
    xs_rows = to_tm(x_sample)
    outs = [[] for _ in range(10)]
    kv_all = None
    for l in range(DEPTH):
        lam_init = 0.8 - 0.6 * math.exp(-0.3 * l)
        mod = _ada_call(c_all, w_ada, b_ada, layer=l)
        mod_p = mod[0:BATCH].reshape(BATCH, 1, 6 * D_MODEL)
        mod_s = jnp.tile(mod[BATCH:BATCH + DEC_BATCH], (DEC_SEQ, 1)).reshape(1, ms, 6 * D_MODEL)
        gmix = g_mix[l].reshape(1, D_MODEL)
        gmlp = g_mlp[l].reshape(1, D_MODEL)
        w_in_bf = w_in[l].astype(BF16)
        wout_bf = w_out[l].astype(BF16)
        wup_bf = w_up[l].astype(BF16)
        wdown_bf = w_down[l].astype(BF16)
        qg = jnp.tile(q_norm[l], QK_WIDTH // QK_DIM).reshape(1, QK_WIDTH)
        kg = jnp.tile(k_norm[l], QK_WIDTH // QK_DIM).reshape(1, QK_WIDTH)
        mix_wts = (_block_diag(pool_w[l]).astype(BF16), pool_scale[l].reshape(1, -1),
                   conv_w[l], conv_b[l].reshape(1, -1),
                   jnp.concatenate([_block_diag(rg_wa[l]), _block_diag(rg_wx[l])], axis=1).astype(BF16),
                   jnp.concatenate([rg_ba[l], rg_bx[l]]).reshape(1, -1),
                   rg_lambda[l].reshape(1, -1))
        lamp = jnp.stack([lambda_q1[l], lambda_k1[l], lambda_q2[l], lambda_k2[l]])
        sub_g = subln[l].reshape(1, V_DIM)

        q, kt_all, v_all, ypool, yrec, hlast, pstate, cstate = _proj_mix_call(
            xp_rows, mod_p, gmix, w_in_bf, cos_p, sin_p, qg, kg, grp, mix_wts, kv_all,
            tm=PROJ_TM, tiles_per_seq=tm_tiles, layer=l, n_layers=DEPTH)
        kv_all = (kt_all, v_all)
        o = _attn_prompt_call(q.reshape(BATCH, SEQ, QK_WIDTH), kt_all,
                              v_all.reshape(DEPTH, BATCH, SEQ * N_HEADS, V_DIM), lamp, sub_g,
                              layer=l, lam_init=lam_init)
        outs[4].append(pstate[:, 16 - POOL_HIST:])
        outs[6].append(cstate[:, 8 - (CONV_WIDTH - 1):])
        outs[8].append(hlast.reshape(BATCH, REC_WIDTH))

        xp_s, xr_s, gate_s, q_s, kt_s, v_s = _proj_call(xs_rows, mod_s, gmix, w_in_bf, cos_s, sin_s, qg, kg, grp,
                                                         tm=ms, tiles_per_seq=1)
        k_s = kt_s[0, 0].T
        hist = (state_pool[l].transpose(1, 0, 2).reshape(POOL_HIST * DEC_BATCH, POOL_WIDTH),
                state_conv[l].transpose(1, 0, 2).reshape((CONV_WIDTH - 1) * DEC_BATCH, REC_WIDTH),
                state_rglru[l])
        ypool_s, yrec_s, hlast_s = _mix_call(xp_s, xr_s, gate_s, hist, mix_wts, n_seq=1, n_t=1,
                                             stride=DEC_BATCH, rows=ms, pos0=PAST_LEN)
        k_bm, v_bm = to_bm(k_s), to_bm(v_s.reshape(ms, ATTN_WIDTH))

        xp_rows, o_s = _out_attn_call(page_table, xp_rows, ypool, yrec, o.reshape(mp, ATTN_WIDTH), mod_p, wout_bf,
                                      gmlp, wup_bf, wdown_bf, to_bm(q_s.astype(F32)), k_bm, v_bm, lamp, sub_g,
                                      cache_kt, cache_vr, tm=OUT_TM, tiles_per_seq=SEQ // OUT_TM, layer=l,
                                      lam_init=lam_init)
        xs_rows = _out_call(xs_rows, ypool_s, yrec_s, to_tm(o_s), mod_s, wout_bf, gmlp,
                            wup_bf, wdown_bf, tm=ms, tiles_per_seq=1)
        outs[2].append(k_bm.reshape(DEC_BATCH, DEC_SEQ, N_HEADS, 2, QK_DIM))
        outs[3].append(v_bm.reshape(DEC_BATCH, DEC_SEQ, N_HEADS, V_DIM))
        outs[5].append(jnp.concatenate([state_pool[l][:, DEC_SEQ:], to_bm(xp_s)], axis=1))
        outs[7].append(to_bm(xr_s)[:, DEC_SEQ - (CONV_WIDTH - 1):])
        outs[9].append(hlast_s.reshape(DEC_BATCH, REC_WIDTH))

    kt_all, v_all = kv_all
    new_k_prompt = kt_all.reshape(DEPTH, BATCH, N_HEADS, 2, QK_DIM, SEQ).transpose(0, 1, 5, 2, 3, 4)
    new_v_prompt = v_all.reshape(DEPTH, BATCH, SEQ, N_HEADS, V_DIM)
    stacked = [jnp.stack(o) for o in outs[2:]]
    return (xp_rows.reshape(BATCH, SEQ, D_MODEL), to_bm(xs_rows), new_k_prompt, new_v_prompt) + tuple(stacked)
```

```python
import functools
import math

import jax
import jax.numpy as jnp
import numpy as np
from jax import lax
from jax.experimental import pallas as pl
from jax.experimental.pallas import tpu as pltpu

D_MODEL = 1024
BATCH = 4
SEQ = 4096
DEPTH = 2
DEC_BATCH = 32
DEC_SEQ = 4
PAST_LEN = 16384
PAGE_SIZE = 128
N_PAGES = PAST_LEN // PAGE_SIZE

POOL_WIDTH = 256
POOL_WINDOWS = (2, 4, 8, 16)
POOL_GROUP = 64
POOL_HIST = 15
REC_WIDTH = 256
CONV_WIDTH = 4
RGLRU_C = 8.0
N_HEADS = 4
QK_DIM = 64
V_DIM = 128
ATTN_WIDTH = N_HEADS * V_DIM
ROPE_DIM = 16
ROPE_THETA = 500000.0
ATTN_SCALE = 1.0 / math.sqrt(QK_DIM)
Q_PRESCALE = ATTN_SCALE * math.log2(math.e)
QK_WIDTH = N_HEADS * 2 * QK_DIM
IN_WIDTH = POOL_WIDTH + 2 * REC_WIDTH + 2 * QK_WIDTH + ATTN_WIDTH
D_FF = 4 * D_MODEL
EPS = 1e-6

F32 = jnp.float32
BF16 = jnp.bfloat16
NEG_BIG = -1e30
V7X_VMEM_LIMIT = 56 * 1024 * 1024

PROJ_TM = 512
OUT_TM = BATCH * SEQ // DEC_BATCH
ATTN_Q_CHUNK = 256
PAGE_GROUP = 8
N_GROUPS = N_PAGES // PAGE_GROUP
N_SLOTS = 4
GROUPS_PER_FF_CHUNK = 2
N_FF_CHUNKS = N_GROUPS // GROUPS_PER_FF_CHUNK
FF_CHUNK = D_FF // N_FF_CHUNKS


def _bdot(a, b):
    return jnp.dot(a.astype(BF16), b.astype(BF16), preferred_element_type=F32)


def _params(n_axes):
    return pltpu.CompilerParams(dimension_semantics=("arbitrary",) * n_axes,
                                vmem_limit_bytes=V7X_VMEM_LIMIT)


def _ada_kernel(c_ref, w_ref, b_ref, o_ref):
    c = c_ref[...]
    o_ref[...] = _bdot(c * jax.nn.sigmoid(c), w_ref[...]) + b_ref[...]


def _ada_call(c_all, w_ada, b_ada, *, layer):
    rows = c_all.shape[0]
    tn = 1536
    return pl.pallas_call(
        _ada_kernel,
        grid=(6 * D_MODEL // tn,),
        in_specs=[pl.BlockSpec((rows, D_MODEL), lambda j: (0, 0)),
                  pl.BlockSpec((None, D_MODEL, tn), lambda j: (layer, 0, j)),
                  pl.BlockSpec((None, 1, tn), lambda j: (layer, 0, j))],
        out_specs=pl.BlockSpec((rows, tn), lambda j: (0, j)),
        out_shape=jax.ShapeDtypeStruct((rows, 6 * D_MODEL), F32),
        compiler_params=_params(1),
        name="ada",
    )(c_all, w_ada, b_ada.reshape(DEPTH, 1, 6 * D_MODEL))


def _proj_body(x_ref, mod_ref, gmix_ref, w_ref, cos_ref, sin_ref, qg_ref, kg_ref, grp_ref,
               xp_ref, xr_ref, gate_ref, q_ref, kt_ref, v_ref):
    x = x_ref[...]
    y = x * lax.rsqrt(jnp.mean(x * x, axis=-1, keepdims=True) + EPS) * gmix_ref[...]
    sh1 = mod_ref[:, 0:D_MODEL]
    sc1 = mod_ref[:, D_MODEL:2 * D_MODEL]
    h = y * (1.0 + sc1) + sh1
    proj = _bdot(h, w_ref[...])
    xp_ref[...] = proj[:, 0:256]
    xr_ref[...] = proj[:, 256:512]
    gate_ref[...] = proj[:, 512:768]
    tm = x.shape[0]
    for slot in range(v_ref.shape[0]):
        for hd in range(N_HEADS):
            v_ref[slot, pl.ds(hd, tm, stride=N_HEADS), :] = proj[:, 1792 + hd * V_DIM:1792 + (hd + 1) * V_DIM]

    cos = jnp.concatenate([cos_ref[...]] * N_HEADS, axis=1)
    sin = jnp.concatenate([sin_ref[...]] * N_HEADS, axis=1)
    lane = lax.broadcasted_iota(jnp.int32, cos.shape, 1) % QK_DIM
    grp = grp_ref[...]

    def norm_rope(z, g):
        ssq = _bdot(z * z, grp)
        zn = z * lax.rsqrt(ssq * (1.0 / QK_DIM) + EPS) * g
        half = ROPE_DIM // 2
        swapped = jnp.where(lane < half, pltpu.roll(zn, QK_WIDTH - half, 1), pltpu.roll(zn, half, 1))
        return zn * cos + swapped * sin

    q = norm_rope(proj[:, 768:1280], qg_ref[...])
    q_ref[...] = (q * Q_PRESCALE).astype(q_ref.dtype)
    kt = norm_rope(proj[:, 1280:1792], kg_ref[...]).T
    for slot in range(kt_ref.shape[0]):
        kt_ref[slot] = kt


def _proj_kernel(*refs):
    _proj_body(*refs)


def _proj_io(x, mod3, gmix, w_in_bf, cos_t, sin_t, qg, kg, grp, *, tm, tiles_per_seq):
    mod_rows = mod3.shape[1]
    full = lambda a: pl.BlockSpec(a.shape, lambda i: (0,) * a.ndim)
    ins = [x, mod3, gmix, w_in_bf, cos_t, sin_t, qg, kg, grp]
    in_specs = [pl.BlockSpec((tm, D_MODEL), lambda i: (i, 0)),
                pl.BlockSpec((None, mod_rows, 6 * D_MODEL), lambda i: (i // tiles_per_seq, 0, 0)),
                full(gmix), full(w_in_bf),
                pl.BlockSpec((tm, V_DIM), lambda i: (i % tiles_per_seq, 0)),
                pl.BlockSpec((tm, V_DIM), lambda i: (i % tiles_per_seq, 0)),
                full(qg), full(kg), full(grp)]
    return ins, in_specs


def _kv_out(m, tm, tiles_per_seq, layer, n_layers):
    n_seq = m // (tm * tiles_per_seq)
    seq_len = tm * tiles_per_seq
    slots, first = (n_layers, 0) if layer == 0 else (1, layer)
    specs = [pl.BlockSpec((slots, None, QK_WIDTH, tm), lambda i: (first, i // tiles_per_seq, 0, i % tiles_per_seq)),
             pl.BlockSpec((slots, tm * N_HEADS, V_DIM), lambda i: (first, i, 0))]
    shapes = [jax.ShapeDtypeStruct((n_layers, n_seq, QK_WIDTH, seq_len), F32),
              jax.ShapeDtypeStruct((n_layers, m * N_HEADS, V_DIM), F32)]
    return specs, shapes


def _proj_call(x, mod3, gmix, w_in_bf, cos_t, sin_t, qg, kg, grp, *, tm, tiles_per_seq):
    m = x.shape[0]
    ins, in_specs = _proj_io(x, mod3, gmix, w_in_bf, cos_t, sin_t, qg, kg, grp, tm=tm, tiles_per_seq=tiles_per_seq)
    kv_specs, kv_shapes = _kv_out(m, tm, tiles_per_seq, 0, 1)
    row = lambda w: pl.BlockSpec((tm, w), lambda i: (i, 0))
    return pl.pallas_call(
        _proj_kernel,
        grid=(m // tm,),
        in_specs=in_specs,
        out_specs=[row(256), row(256), row(256), row(512)] + kv_specs,
        out_shape=[jax.ShapeDtypeStruct((m, 256), F32)] * 3 + [jax.ShapeDtypeStruct((m, 512), BF16)] + kv_shapes,
        compiler_params=_params(1),
        name="proj",
    )(*ins)


def _mix_kernel(*refs, stride, rows, pos0, has_hist):
    hist = refs[3:6] if has_hist else None
    rest = refs[6:] if has_hist else refs[3:]
    t = pl.program_id(1)
    _mix_prepare(t, hist, *rest[-3:], stride=stride, rows=rows)
    _mix_body(t, refs[0], refs[1], refs[2], *rest, stride=stride, rows=rows, pos0=pos0)


def _mix_prepare(t, hist, pext, cext, hcar, *, stride, rows):
    has_hist = hist is not None
    if has_hist:
        hp_ref, hc_ref, h0_ref = hist
    hp_rows = 16 * stride
    hc_rows = 8 * stride

    @pl.when(t == 0)
    def _():
        pext[0:hp_rows, :] = jnp.zeros((hp_rows, POOL_WIDTH), F32)
        cext[0:hc_rows, :] = jnp.zeros((hc_rows, REC_WIDTH), F32)
        if has_hist:
            pext[stride:hp_rows, :] = hp_ref[...]
            cext[(8 - (CONV_WIDTH - 1)) * stride:hc_rows, :] = hc_ref[...]
            hcar[...] = h0_ref[...]
        else:
            hcar[...] = jnp.zeros((stride, REC_WIDTH), F32)

    @pl.when(t > 0)
    def _():
        pext[0:hp_rows, :] = pext[rows:rows + hp_rows, :]
        cext[0:hc_rows, :] = cext[rows:rows + hc_rows, :]


def _mix_body(t, xp_ref, xr_ref, gate_ref, poolw_ref, pscale_ref, convw_ref, convb_ref,
              wax_ref, bax_ref, lam_ref, ypool_ref, yrec_ref, hlast_ref, pext, cext, hcar, *, stride, rows, pos0,
              scan_scratch=None):
    tt = rows // stride
    hp_rows = 16 * stride
    hc_rows = 8 * stride
    pext[hp_rows:hp_rows + rows, :] = xp_ref[...]
    cext[hc_rows:hc_rows + rows, :] = xr_ref[...]

    tidx = lax.broadcasted_iota(jnp.int32, (rows, 128), 0) // stride
    lane = lax.broadcasted_iota(jnp.int32, (rows, 128), 1)
    pos1 = (pos0 + 1 + t * tt + tidx).astype(F32)

    ext = pext[...]
    s2 = ext + pltpu.roll(ext, stride, 0)
    s4 = s2 + pltpu.roll(s2, 2 * stride, 0)
    s8 = s4 + pltpu.roll(s4, 4 * stride, 0)
    s16 = s8 + pltpu.roll(s8, 8 * stride, 0)
    new = slice(hp_rows, hp_rows + rows)

    def window_means(lo, small, big, w_small):
        cnt = jnp.where(lane < POOL_GROUP, jnp.minimum(float(w_small), pos1),
                        jnp.minimum(float(2 * w_small), pos1))
        return jnp.where(lane < POOL_GROUP, small[new, lo:lo + 128], big[new, lo:lo + 128]) / cnt

    d = jnp.concatenate([window_means(0, s2, s4, 2), window_means(128, s8, s16, 8)], axis=1) - ext[new, :]
    ypool_ref[...] = (_bdot(d, poolw_ref[...]) * pscale_ref[...]).astype(ypool_ref.dtype)

    y = convb_ref[...]
    for tap in range(CONV_WIDTH):
        off = hc_rows - (CONV_WIDTH - 1 - tap) * stride
        y = y + cext[off:off + rows, :] * convw_ref[tap:tap + 1, :]
    rg = _bdot(y, wax_ref[...]) + bax_ref[...]
    r = jax.nn.sigmoid(rg[:, 0:REC_WIDTH])
    i = jax.nn.sigmoid(rg[:, REC_WIDTH:2 * REC_WIDTH])
    nl = -lam_ref[...]
    softplus = jnp.maximum(nl, 0.0) + jnp.log1p(jnp.exp(-jnp.abs(nl)))
    a = jnp.exp(-RGLRU_C * r * softplus)
    b = jnp.sqrt(1.0 - a * a) * (i * y)

    def doubling_scan(a, b, idx, n, shift):
        step = 1
        while step < n:
            a_prev = pltpu.roll(a, step * shift, 0)
            b_prev = pltpu.roll(b, step * shift, 0)
            valid = idx >= step
            b = jnp.where(valid, a * b_prev + b, b)
            a = jnp.where(valid, a * a_prev, a)
            step *= 2
        return a, b

    hc = hcar[...]
    row = lax.broadcasted_iota(jnp.int32, (rows, REC_WIDTH), 0)
    if scan_scratch is None:
        a, b = doubling_scan(a, b, row // stride, tt, stride)
        carry = jnp.broadcast_to(hc, (rows, REC_WIDTH)) if stride == 1 else jnp.concatenate([hc] * tt, axis=0)
    else:
        a_s, b_s, hin_s = scan_scratch
        n_groups = rows // 8
        a, b = doubling_scan(a, b, row % 8, 8, 1)
        halves = range(REC_WIDTH // 128)

        def put(ref, val):
            for k in halves:
                ref[k] = val[:, 128 * k:128 * (k + 1)]

        put(a_s, a)
        put(b_s, b)
        ga = jnp.concatenate([a_s[k, pl.ds(7, n_groups, stride=8), :] for k in halves], axis=1)
        gb = jnp.concatenate([b_s[k, pl.ds(7, n_groups, stride=8), :] for k in halves], axis=1)
        gidx = lax.broadcasted_iota(jnp.int32, (n_groups, REC_WIDTH), 0)
        ga, gb = doubling_scan(ga, gb, gidx, n_groups, 1)
        after = gb + ga * hc
        put(hin_s, jnp.where(gidx == 0, jnp.broadcast_to(hc, after.shape), pltpu.roll(after, 1, 0)))
        carry = jnp.concatenate(
            [jnp.concatenate([hin_s[k, pl.ds(g, 8, stride=0), :] for g in range(n_groups)], axis=0) for k in halves],
            axis=1)
    h = b + a * carry
    h_end = h[rows - stride:rows, :]
    hcar[...] = h_end
    hlast_ref[...] = h_end

    g = gate_ref[...]
    gelu = 0.5 * g * (1.0 + jnp.tanh(math.sqrt(2.0 / math.pi) * (g + 0.044715 * (g * g * g))))
    yrec_ref[...] = (h * gelu).astype(yrec_ref.dtype)


def _mix_call(xp, xr, gate, hist, wts, *, n_seq, n_t, stride, rows, pos0):
    m = xp.shape[0]
    has_hist = hist is not None
    row = pl.BlockSpec((rows, 256), lambda b, t: (b * n_t + t, 0))
    full = lambda a: pl.BlockSpec(a.shape, lambda b, t: (0,) * a.ndim)
    ins = [xp, xr, gate] + (list(hist) if has_hist else []) + list(wts)
    in_specs = [row, row, row] + [full(a) for a in ins[3:]]
    return pl.pallas_call(
        functools.partial(_mix_kernel, stride=stride, rows=rows, pos0=pos0, has_hist=has_hist),
        grid=(n_seq, n_t),
        in_specs=in_specs,
        out_specs=[row, row, pl.BlockSpec((None, stride, REC_WIDTH), lambda b, t: (b, 0, 0))],
        out_shape=[jax.ShapeDtypeStruct((m, 256), BF16), jax.ShapeDtypeStruct((m, 256), BF16),
                   jax.ShapeDtypeStruct((n_seq, stride, REC_WIDTH), F32)],
        scratch_shapes=[pltpu.VMEM((16 * stride + rows, POOL_WIDTH), F32),
                        pltpu.VMEM((8 * stride + rows, REC_WIDTH), F32),
                        pltpu.VMEM((stride, REC_WIDTH), F32)],
        compiler_params=_params(2),
        name="mix",
    )(*ins)


N_MIX_WTS = 7


def _proj_mix_kernel(*refs, rows, tiles_per_seq, aliased):
    proj_in = refs[0:9]
    mix_wts = refs[9:9 + N_MIX_WTS]
    outs = refs[9 + N_MIX_WTS + (2 if aliased else 0):]
    q_ref, kt_ref, v_ref, ypool_ref, yrec_ref, hlast_ref, pstate_ref, cstate_ref = outs[0:8]
    xp_s, xr_s, gate_s, pext, cext, hcar, scan_a, scan_b, scan_h = outs[8:]
    t = pl.program_id(0) % tiles_per_seq
    _mix_prepare(t, None, pext, cext, hcar, stride=1, rows=rows)
    _proj_body(*proj_in, xp_s, xr_s, gate_s, q_ref, kt_ref, v_ref)
    _mix_body(t, xp_s, xr_s, gate_s, *mix_wts, ypool_ref, yrec_ref, hlast_ref, pext, cext, hcar,
              stride=1, rows=rows, pos0=0, scan_scratch=(scan_a, scan_b, scan_h))
    pstate_ref[...] = pext[rows:rows + 16, :]
    cstate_ref[...] = cext[rows:rows + 8, :]


def _proj_mix_call(x, mod3, gmix, w_in_bf, cos_t, sin_t, qg, kg, grp, mix_wts, kv_prev, *,
                   tm, tiles_per_seq, layer, n_layers):
    m = x.shape[0]
    n_seq = m // (tm * tiles_per_seq)
    ins, in_specs = _proj_io(x, mod3, gmix, w_in_bf, cos_t, sin_t, qg, kg, grp, tm=tm, tiles_per_seq=tiles_per_seq)
    full = lambda a: pl.BlockSpec(a.shape, lambda i: (0,) * a.ndim)
    ins += list(mix_wts)
    in_specs += [full(a) for a in mix_wts]
    aliases = {}
    if kv_prev is not None:
        aliases = {len(ins): 1, len(ins) + 1: 2}
        ins += list(kv_prev)
        in_specs += [pl.BlockSpec(memory_space=pl.ANY)] * 2
    kv_specs, kv_shapes = _kv_out(m, tm, tiles_per_seq, layer, n_layers)
    row = lambda w: pl.BlockSpec((tm, w), lambda i: (i, 0))
    per_seq = lambda r: pl.BlockSpec((None, r, 256), lambda i: (i // tiles_per_seq, 0, 0))
    return pl.pallas_call(
        functools.partial(_proj_mix_kernel, rows=tm, tiles_per_seq=tiles_per_seq, aliased=kv_prev is not None),
        grid=(m // tm,),
        in_specs=in_specs,
        out_specs=[row(512)] + kv_specs + [row(256), row(256), per_seq(1), per_seq(16), per_seq(8)],
        out_shape=[jax.ShapeDtypeStruct((m, 512), BF16)] + kv_shapes
                  + [jax.ShapeDtypeStruct((m, 256), BF16), jax.ShapeDtypeStruct((m, 256), BF16),
                     jax.ShapeDtypeStruct((n_seq, 1, REC_WIDTH), F32),
                     jax.ShapeDtypeStruct((n_seq, 16, POOL_WIDTH), F32),
                     jax.ShapeDtypeStruct((n_seq, 8, REC_WIDTH), F32)],
        scratch_shapes=[pltpu.VMEM((tm, POOL_WIDTH), F32), pltpu.VMEM((tm, REC_WIDTH), F32),
                        pltpu.VMEM((tm, REC_WIDTH), F32),
                        pltpu.VMEM((16 + tm, POOL_WIDTH), F32), pltpu.VMEM((8 + tm, REC_WIDTH), F32),
                        pltpu.VMEM((1, REC_WIDTH), F32),
                        pltpu.VMEM((REC_WIDTH // 128, tm, 128), F32), pltpu.VMEM((REC_WIDTH // 128, tm, 128), F32),
                        pltpu.VMEM((REC_WIDTH // 128, tm // 8, 128), F32)],
        input_output_aliases=aliases,
        compiler_params=_params(1),
        name="proj_mix",
    )(*ins)


def _diff_lambda(lamp_ref, lam_init):
    lp = lamp_ref[...]
    s1 = jnp.sum(lp[0:1, :] * lp[1:2, :], axis=-1, keepdims=True)
    s2 = jnp.sum(lp[2:3, :] * lp[3:4, :], axis=-1, keepdims=True)
    return jnp.exp(s1) - jnp.exp(s2) + lam_init


def _subln(o, g, lam_init):
    return o * lax.rsqrt(jnp.mean(o * o, axis=-1, keepdims=True) + EPS) * g * (1.0 - lam_init)


def _attn_prompt_kernel(q_ref, k_ref, v_ref, lamp_ref, subln_ref, o_ref, kb, vb, *, chunk, lam_init):
    kb[...] = k_ref[...].astype(BF16)
    head = pl.program_id(1)
    vb[:, 0:V_DIM] = v_ref[pl.ds(head, SEQ, stride=N_HEADS), :].astype(BF16)
    vb[:, V_DIM:2 * V_DIM] = jnp.ones((SEQ, V_DIM), BF16)
    lam = _diff_lambda(lamp_ref, lam_init)
    gain = subln_ref[...]
    lane = lax.broadcasted_iota(jnp.int32, (chunk, 2 * QK_DIM), 1)
    r = lax.broadcasted_iota(jnp.int32, (2 * chunk, chunk), 0) % chunk
    c = lax.broadcasted_iota(jnp.int32, (2 * chunk, chunk), 1)
    causal = c <= r
    for r0 in range(0, SEQ, chunk):
        nk = r0 + chunk
        q = q_ref[r0:r0 + chunk, :]
        zero = jnp.zeros_like(q)
        q2 = jnp.concatenate([jnp.where(lane < QK_DIM, q, zero), jnp.where(lane >= QK_DIM, q, zero)], axis=0)
        s = jnp.dot(q2, kb[:, 0:nk], preferred_element_type=F32)
        tail = jnp.where(causal, s[:, r0:nk], NEG_BIG)
        s = tail if r0 == 0 else jnp.concatenate([s[:, 0:r0], tail], axis=1)
        m = jnp.max(s, axis=-1, keepdims=True)
        e = jnp.exp2(s - m).astype(BF16)
        pv = jnp.dot(e, vb[0:nk, :], preferred_element_type=F32)
        a0 = pv[0:chunk, :]
        a1 = pv[chunk:2 * chunk, :]
        o = a0[:, 0:V_DIM] / a0[:, V_DIM:2 * V_DIM] - lam * (a1[:, 0:V_DIM] / a1[:, V_DIM:2 * V_DIM])
        o_ref[r0:r0 + chunk, :] = _subln(o, gain, lam_init).astype(o_ref.dtype)


def _attn_prompt_call(q, kt_all, v_all, lamp, subln, *, layer, lam_init):
    qspec = pl.BlockSpec((None, SEQ, V_DIM), lambda b, h: (b, 0, h))
    full = lambda a: pl.BlockSpec(a.shape, lambda b, h: (0,) * a.ndim)
    return pl.pallas_call(
        functools.partial(_attn_prompt_kernel, chunk=ATTN_Q_CHUNK, lam_init=lam_init),
        grid=(BATCH, N_HEADS),
        in_specs=[qspec,
                  pl.BlockSpec((None, None, 2 * QK_DIM, SEQ), lambda b, h: (layer, b, h, 0)),
                  pl.BlockSpec((None, None, SEQ * N_HEADS, V_DIM), lambda b, h: (layer, b, 0, 0)),
                  full(lamp), full(subln)],
        out_specs=qspec,
        out_shape=jax.ShapeDtypeStruct((BATCH, SEQ, ATTN_WIDTH), BF16),
        scratch_shapes=[pltpu.VMEM((2 * QK_DIM, SEQ), BF16),
                        pltpu.VMEM((SEQ, 2 * V_DIM), BF16)],
        compiler_params=_params(2),
        name="attn_prompt",
    )(q, kt_all, v_all, lamp, subln)


N_SROWS = DEC_SEQ * N_HEADS * 2


def _sample_query_rows(q):
    r = lax.broadcasted_iota(jnp.int32, (N_SROWS, QK_WIDTH), 0)
    g = lax.broadcasted_iota(jnp.int32, (N_SROWS, QK_WIDTH), 1) // QK_DIM
    rows = jnp.zeros((N_SROWS, QK_WIDTH), F32)
    for t in range(DEC_SEQ):
        rows = jnp.where(r % DEC_SEQ == t, jnp.broadcast_to(q[t:t + 1, :], (N_SROWS, QK_WIDTH)), rows)
    return jnp.where(g == r // DEC_SEQ, rows, 0.0).astype(BF16)


def _sample_softmax_update(state, s, v_heads):
    m_old, l_old, acc_old = state
    m_new = jnp.maximum(m_old, jnp.max(s, axis=-1, keepdims=True))
    alpha = jnp.exp2(m_old - m_new)
    e = jnp.concatenate([jnp.exp2(s[:, j:j + 128] - m_new) for j in range(0, s.shape[1], 128)], axis=1)
    l_new = alpha * l_old + jnp.sum(e, axis=-1, keepdims=True)
    eb = e.astype(BF16)
    pv = [jnp.dot(eb[8 * h:8 * h + 8, :], v_heads[h], preferred_element_type=F32) for h in range(N_HEADS)]
    return m_new, l_new, alpha * acc_old + jnp.concatenate(pv, axis=0)


def _sample_new_rows_and_finish(state, qbd, kn_ref, vn_ref, knp_ref, vnp_ref, lamp_ref, subln_ref, os_ref, lam_init):
    knp_ref[...] = jnp.zeros((PAGE_SIZE, QK_WIDTH), F32)
    vnp_ref[...] = jnp.zeros((PAGE_SIZE, ATTN_WIDTH), F32)
    knp_ref[0:DEC_SEQ, :] = kn_ref[...]
    vnp_ref[0:DEC_SEQ, :] = vn_ref[...]
    knp = knp_ref[...].astype(BF16)
    vnp = vnp_ref[...].astype(BF16)
    s_new = lax.dot_general(qbd, knp, (((1,), (1,)), ((), ())), preferred_element_type=F32)
    row_t = lax.broadcasted_iota(jnp.int32, (N_SROWS, PAGE_SIZE), 0) % DEC_SEQ
    col = lax.broadcasted_iota(jnp.int32, (N_SROWS, PAGE_SIZE), 1)
    state = _sample_softmax_update(state, jnp.where(col <= row_t, s_new, NEG_BIG),
                                   [vnp[:, h * V_DIM:(h + 1) * V_DIM] for h in range(N_HEADS)])
    _, l, acc = state
    on = acc / l
    lam = _diff_lambda(lamp_ref, lam_init)
    gain = subln_ref[...]
    for h in range(N_HEADS):
        blk = on[8 * h:8 * h + 8, :]
        diff = blk - lam * pltpu.roll(blk, DEC_SEQ, 0)
        os_ref[:, h * V_DIM:(h + 1) * V_DIM] = _subln(diff, gain, lam_init)[0:DEC_SEQ, :]


def _residual_and_mlp_input(x_ref, yp_ref, yr_ref, o_ref, mod_ref, wout_ref, gmlp_ref):
    mix = (jnp.dot(yp_ref[...].astype(BF16), wout_ref[0:256, :], preferred_element_type=F32)
           + jnp.dot(yr_ref[...].astype(BF16), wout_ref[256:512, :], preferred_element_type=F32)
           + jnp.dot(o_ref[...].astype(BF16), wout_ref[512:1024, :], preferred_element_type=F32))
    g1 = mod_ref[:, 2 * D_MODEL:3 * D_MODEL]
    sh2 = mod_ref[:, 3 * D_MODEL:4 * D_MODEL]
    sc2 = mod_ref[:, 4 * D_MODEL:5 * D_MODEL]
    x1 = x_ref[...] + g1 * mix
    y = x1 * lax.rsqrt(jnp.mean(x1 * x1, axis=-1, keepdims=True) + EPS) * gmlp_ref[...]
    return x1, (y * (1.0 + sc2) + sh2).astype(BF16)


def _mlp_chunk(h2, wup_ref, wdown_ref, c):
    cols = slice(c * FF_CHUNK, (c + 1) * FF_CHUNK)
    up = jnp.dot(h2, wup_ref[:, cols], preferred_element_type=F32)
    act = jnp.square(jnp.maximum(up, 0.0)).astype(BF16)
    return jnp.dot(act, wdown_ref[cols, :], preferred_element_type=F32)


def _out_kernel(x_ref, yp_ref, yr_ref, o_ref, mod_ref, wout_ref, gmlp_ref, wup_ref, wdown_ref, xo_ref):
    x1, h2 = _residual_and_mlp_input(x_ref, yp_ref, yr_ref, o_ref, mod_ref, wout_ref, gmlp_ref)
    mlp = _mlp_chunk(h2, wup_ref, wdown_ref, 0)
    for c in range(1, N_FF_CHUNKS):
        mlp = mlp + _mlp_chunk(h2, wup_ref, wdown_ref, c)
    xo_ref[...] = x1 + mod_ref[:, 5 * D_MODEL:6 * D_MODEL] * mlp


def _out_call(x, yp, yr, o, mod3, wout_bf, gmlp, wup_bf, wdown_bf, *, tm, tiles_per_seq):
    m = x.shape[0]
    mod_rows = mod3.shape[1]
    row = lambda w: pl.BlockSpec((tm, w), lambda i: (i, 0))
    full = lambda a: pl.BlockSpec(a.shape, lambda i: (0,) * a.ndim, pipeline_mode=pl.Buffered(1))
    return pl.pallas_call(
        _out_kernel,
        grid=(m // tm,),
        in_specs=[row(D_MODEL), row(256), row(256), row(512),
                  pl.BlockSpec((None, mod_rows, 6 * D_MODEL), lambda i: (i // tiles_per_seq, 0, 0)),
                  full(wout_bf), full(gmlp), full(wup_bf), full(wdown_bf)],
        out_specs=row(D_MODEL),
        out_shape=jax.ShapeDtypeStruct((m, D_MODEL), F32),
        compiler_params=_params(1),
        name="out",
    )(x, yp, yr, o, mod3, wout_bf, gmlp, wup_bf, wdown_bf)


def _out_attn_kernel(pt_ref, x_ref, yp_ref, yr_ref, o_ref, mod_ref, wout_ref, gmlp_ref, wup_ref, wdown_ref,
                     q_ref, kn_ref, vn_ref, lamp_ref, subln_ref, ck_ref, cv_ref,
                     xo_ref, os_ref,
                     kbuf, vbuf, sem, h2_s, mlp_s, qbd_s, knp_s, vnp_s, *, layer, lam_init):
    i = pl.program_id(0)
    n_steps = pl.num_programs(0)

    def page_copies(seq, group, slot):
        copies = []
        for j in range(PAGE_GROUP):
            page = pt_ref[seq, group * PAGE_GROUP + j]
            copies.append(pltpu.make_async_copy(ck_ref.at[layer, page], kbuf.at[slot, j], sem.at[0, slot]))
            copies.append(pltpu.make_async_copy(cv_ref.at[layer, page], vbuf.at[slot, j], sem.at[1, slot]))
        return copies

    @pl.when(i == 0)
    def _():
        for slot in range(N_SLOTS):
            for c in page_copies(0, slot, slot):
                c.start()

    x1, h2 = _residual_and_mlp_input(x_ref, yp_ref, yr_ref, o_ref, mod_ref, wout_ref, gmlp_ref)
    xo_ref[...] = x1
    h2_s[...] = h2
    mlp_s[...] = jnp.zeros(mlp_s.shape, F32)
    qbd_s[...] = _sample_query_rows(q_ref[...])

    def page_group(slot, state):
        kt = jnp.concatenate([kbuf[slot, j].reshape(QK_WIDTH, PAGE_SIZE).astype(BF16)
                              for j in range(PAGE_GROUP)], axis=1)
        s = jnp.dot(qbd_s[...], kt, preferred_element_type=F32)
        v_heads = [jnp.concatenate([vbuf[slot, j, pl.ds(h, PAGE_SIZE, stride=N_HEADS), :].astype(BF16)
                                    for j in range(PAGE_GROUP)], axis=0) for h in range(N_HEADS)]
        return _sample_softmax_update(state, s, v_heads)

    state = (jnp.full((N_SROWS, 128), NEG_BIG, F32), jnp.zeros((N_SROWS, 128), F32),
             jnp.zeros((N_SROWS, V_DIM), F32))
    for group in range(N_GROUPS):
        slot = group % N_SLOTS
        for c in page_copies(i, group, slot):
            c.wait()
        if group % GROUPS_PER_FF_CHUNK == 0:
            mlp_s[...] += _mlp_chunk(h2_s[...], wup_ref, wdown_ref, group // GROUPS_PER_FF_CHUNK)
        state = page_group(slot, state)
        if group + N_SLOTS < N_GROUPS:
            for c in page_copies(i, group + N_SLOTS, slot):
                c.start()
        else:
            @pl.when(i + 1 < n_steps)
            def _():
                for c in page_copies(i + 1, group + N_SLOTS - N_GROUPS, slot):
                    c.start()

    _sample_new_rows_and_finish(state, qbd_s[...], kn_ref, vn_ref, knp_s, vnp_s, lamp_ref, subln_ref, os_ref, lam_init)
    xo_ref[...] = xo_ref[...] + mod_ref[:, 5 * D_MODEL:6 * D_MODEL] * mlp_s[...]


def _out_attn_call(page_table, x, yp, yr, o, mod3, wout_bf, gmlp, wup_bf, wdown_bf,
                   q_bm, k_bm, v_bm, lamp, subln, cache_kt, cache_vr, *, tm, tiles_per_seq, layer, lam_init):
    m = x.shape[0]
    assert m // tm == DEC_BATCH, "one prompt row tile per sample sequence"
    row = lambda w: pl.BlockSpec((tm, w), lambda i, pt: (i, 0))
    full = lambda a: pl.BlockSpec(a.shape, lambda i, pt: (0,) * a.ndim, pipeline_mode=pl.Buffered(1))
    tok = pl.BlockSpec((None, DEC_SEQ, QK_WIDTH), lambda i, pt: (i, 0, 0))
    hbm = pl.BlockSpec(memory_space=pl.ANY)
    grid_spec = pltpu.PrefetchScalarGridSpec(
        num_scalar_prefetch=1,
        grid=(m // tm,),
        in_specs=[row(D_MODEL), row(256), row(256), row(512),
                  pl.BlockSpec((None, 1, 6 * D_MODEL), lambda i, pt: (i // tiles_per_seq, 0, 0)),
                  full(wout_bf), full(gmlp), full(wup_bf), full(wdown_bf),
                  tok, tok, tok, full(lamp), full(subln), hbm, hbm],
        out_specs=[row(D_MODEL), tok],
        scratch_shapes=[pltpu.VMEM((N_SLOTS, PAGE_GROUP, N_HEADS, 2, QK_DIM, PAGE_SIZE), F32),
                        pltpu.VMEM((N_SLOTS, PAGE_GROUP, PAGE_SIZE * N_HEADS, V_DIM), F32),
                        pltpu.SemaphoreType.DMA((2, N_SLOTS)),
                        pltpu.VMEM((tm, D_MODEL), BF16),
                        pltpu.VMEM((tm, D_MODEL), F32),
                        pltpu.VMEM((N_SROWS, QK_WIDTH), BF16),
                        pltpu.VMEM((PAGE_SIZE, QK_WIDTH), F32),
                        pltpu.VMEM((PAGE_SIZE, ATTN_WIDTH), F32)])
    return pl.pallas_call(
        functools.partial(_out_attn_kernel, layer=layer, lam_init=lam_init),
        grid_spec=grid_spec,
        out_shape=[jax.ShapeDtypeStruct((m, D_MODEL), F32),
                   jax.ShapeDtypeStruct((DEC_BATCH, DEC_SEQ, ATTN_WIDTH), F32)],
        compiler_params=_params(1),
        name="out_attn",
    )(page_table, x, yp, yr, o, mod3, wout_bf, gmlp, wup_bf, wdown_bf, q_bm, k_bm, v_bm, lamp, subln,
      cache_kt, cache_vr)


def _block_diag(w):
    g, c, d = w.shape
    eye = jnp.eye(g, dtype=w.dtype)
    return (eye[:, None, :, None] * w[:, :, None, :]).reshape(g * c, g * d)


def _rope_tables(pos):
    half = ROPE_DIM // 2
    freqs = ROPE_THETA ** (-np.arange(half, dtype=np.float64) / half)
    ang = np.asarray(pos, np.float64)[:, None] * freqs[None, :]
    cos, sin = np.cos(ang), np.sin(ang)
    n = ang.shape[0]
    c64 = np.concatenate([cos, cos, np.ones((n, QK_DIM - ROPE_DIM))], axis=1)
    s64 = np.concatenate([-sin, sin, np.zeros((n, QK_DIM - ROPE_DIM))], axis=1)
    return (jnp.asarray(np.concatenate([c64, c64], axis=1), F32),
            jnp.asarray(np.concatenate([s64, s64], axis=1), F32))


def kernel(x_prompt, x_sample, cache_k, cache_v, page_table, state_pool, state_conv, state_rglru,
           c_prompt, c_sample, w_ada, b_ada, g_mix, w_in, pool_w, pool_scale, conv_w, conv_b,
           rg_wa, rg_ba, rg_wx, rg_bx, rg_lambda, q_norm, k_norm, lambda_q1, lambda_k1,
           lambda_q2, lambda_k2, subln, w_out, g_mlp, w_up, w_down):
    n_pool = cache_k.shape[1]
    mp = BATCH * SEQ
    ms = DEC_BATCH * DEC_SEQ
    tm_tiles = SEQ // PROJ_TM

    cos_p, sin_p = _rope_tables(np.arange(SEQ))
    cos_s, sin_s = _rope_tables(PAST_LEN + np.repeat(np.arange(DEC_SEQ), DEC_BATCH))
    grp = _block_diag(jnp.ones((QK_WIDTH // QK_DIM, QK_DIM, QK_DIM), BF16))
    cache_kt = cache_k.transpose(0, 1, 3, 4, 5, 2)
    cache_vr = cache_v.reshape(DEPTH, n_pool, PAGE_SIZE * N_HEADS, V_DIM)
    pad_rows = (-(BATCH + DEC_BATCH)) % 16
    c_all = jnp.concatenate([c_prompt, c_sample, jnp.zeros((pad_rows, D_MODEL), F32)], axis=0)

    def to_tm(a):
        return a.transpose(1, 0, 2).reshape(ms, a.shape[-1])

    def to_bm(a):
        return a.reshape(DEC_SEQ, DEC_BATCH, a.shape[-1]).transpose(1, 0, 2)

    xp_rows = x_prompt.reshape(mp, D_MODEL)
    xs_rows = to_tm(x_sample)
    outs = [[] for _ in range(10)]
    kv_all = None
    for l in range(DEPTH):
        lam_init = 0.8 - 0.6 * math.exp(-0.3 * l)
        mod = _ada_call(c_all, w_ada, b_ada, layer=l)
        mod_p = mod[0:BATCH].reshape(BATCH, 1, 6 * D_MODEL)
        mod_s = jnp.tile(mod[BATCH:BATCH + DEC_BATCH], (DEC_SEQ, 1)).reshape(1, ms, 6 * D_MODEL)
        gmix = g_mix[l].reshape(1, D_MODEL)
        gmlp = g_mlp[l].reshape(1, D_MODEL)
        w_in_bf = w_in[l].astype(BF16)
        wout_bf = w_out[l].astype(BF16)
        wup_bf = w_up[l].astype(BF16)
        wdown_bf = w_down[l].astype(BF16)
        qg = jnp.tile(q_norm[l], QK_WIDTH // QK_DIM).reshape(1, QK_WIDTH)
        kg = jnp.tile(k_norm[l], QK_WIDTH // QK_DIM).reshape(1, QK_WIDTH)
        mix_wts = (_block_diag(pool_w[l]).astype(BF16), pool_scale[l].reshape(1, -1),
                   conv_w[l], conv_b[l].reshape(1, -1),
                   jnp.concatenate([_block_diag(rg_wa[l]), _block_diag(rg_wx[l])], axis=1).astype(BF16),
                   jnp.concatenate([rg_ba[l], rg_bx[l]]).reshape(1, -1),
                   rg_lambda[l].reshape(1, -1))
        lamp = jnp.stack([lambda_q1[l], lambda_k1[l], lambda_q2[l], lambda_k2[l]])
        sub_g = subln[l].reshape(1, V_DIM)

        q, kt_all, v_all, ypool, yrec, hlast, pstate, cstate = _proj_mix_call(
            xp_rows, mod_p, gmix, w_in_bf, cos_p, sin_p, qg, kg, grp, mix_wts, kv_all,
            tm=PROJ_TM, tiles_per_seq=tm_tiles, layer=l, n_layers=DEPTH)
        kv_all = (kt_all, v_all)
        o = _attn_prompt_call(q.reshape(BATCH, SEQ, QK_WIDTH), kt_all,
                              v_all.reshape(DEPTH, BATCH, SEQ * N_HEADS, V_DIM), lamp, sub_g,
                              layer=l, lam_init=lam_init)
        outs[4].append(pstate[:, 16 - POOL_HIST:])
        outs[6].append(cstate[:, 8 - (CONV_WIDTH - 1):])
        outs[8].append(hlast.reshape(BATCH, REC_WIDTH))

        xp_s, xr_s, gate_s, q_s, kt_s, v_s = _proj_call(xs_rows, mod_s, gmix, w_in_bf, cos_s, sin_s, qg, kg, grp,
                                                         tm=ms, tiles_per_seq=1)
        k_s = kt_s[0, 0].T
        hist = (state_pool[l].transpose(1, 0, 2).reshape(POOL_HIST * DEC_BATCH, POOL_WIDTH),
                state_conv[l].transpose(1, 0, 2).reshape((CONV_WIDTH - 1) * DEC_BATCH, REC_WIDTH),
                state_rglru[l])
        ypool_s, yrec_s, hlast_s = _mix_call(xp_s, xr_s, gate_s, hist, mix_wts, n_seq=1, n_t=1,
                                             stride=DEC_BATCH, rows=ms, pos0=PAST_LEN)
        k_bm, v_bm = to_bm(k_s), to_bm(v_s.reshape(ms, ATTN_WIDTH))

        xp_rows, o_s = _out_attn_call(page_table, xp_rows, ypool, yrec, o.reshape(mp, ATTN_WIDTH), mod_p, wout_bf,
                                      gmlp, wup_bf, wdown_bf, to_bm(q_s.astype(F32)), k_bm, v_bm, lamp, sub_g,
                                      cache_kt, cache_vr, tm=OUT_TM, tiles_per_seq=SEQ // OUT_TM, layer=l,
                                      lam_init=lam_init)
        xs_rows = _out_call(xs_rows, ypool_s, yrec_s, to_tm(o_s), mod_s, wout_bf, gmlp,
                            wup_bf, wdown_bf, tm=ms, tiles_per_seq=1)
        outs[2].append(k_bm.reshape(DEC_BATCH, DEC_SEQ, N_HEADS, 2, QK_DIM))
        outs[3].append(v_bm.reshape(DEC_BATCH, DEC_SEQ, N_HEADS, V_DIM))
        outs[5].append(jnp.concatenate([state_pool[l][:, DEC_SEQ:], to_bm(xp_s)], axis=1))
        outs[7].append(to_bm(xr_s)[:, DEC_SEQ - (CONV_WIDTH - 1):])
        outs[9].append(hlast_s.reshape(DEC_BATCH, REC_WIDTH))

    kt_all, v_all = kv_all
    new_k_prompt = kt_all.reshape(DEPTH, BATCH, N_HEADS, 2, QK_DIM, SEQ).transpose(0, 1, 5, 2, 3, 4)
    new_v_prompt = v_all.reshape(DEPTH, BATCH, SEQ, N_HEADS, V_DIM)
    stacked = [jnp.stack(o) for o in outs[2:]]
    return (xp_rows.reshape(BATCH, SEQ, D_MODEL), to_bm(xs_rows), new_k_prompt, new_v_prompt) + tuple(stacked)
```
